```python
import math
import jax
import jax.numpy as jnp
from jax import lax
import numpy as np

D_MODEL = 1024
BATCH = 16
SEQ = 4096
DEPTH = 4

F32 = jnp.float32
N_MIXERS = 4
N_META = 16
EXPAND = 2
D_INNER = EXPAND * D_MODEL
RMS_EPS = 1e-6
LN_EPS = 1e-5
ROPE_THETA = 10000.0
Q_BLOCK = 128

HY_SHORT = 3
HY_EMB = 33
HY_BANDS = (HY_EMB - 1) // 2
HY_HIDDEN = 64
HY_SHORT_DECAY_PCT = 0.3
HY_LONG_DECAY_PCT = 1.5
HY_DECAY_TARGET = 1e-2
HY_FILTER_EPS = 1e-6

DA_HEAD_DIM = 64
DA_V_DIM = 2 * DA_HEAD_DIM
DA_HEADS = D_INNER // DA_V_DIM
DA_QK = DA_HEADS * 2 * DA_HEAD_DIM
DA_NORM_EPS = 1e-5

GDN_HEAD_DIM = 128
GDN_V_HEADS = D_INNER // GDN_HEAD_DIM
GDN_K_HEADS = GDN_V_HEADS // 2
GDN_QK = GDN_K_HEADS * GDN_HEAD_DIM
GDN_SHORT = 3
GDN_CHUNK = 64

CF_WIDTH = 31

N_A = (DEPTH + 3) // N_MIXERS
N_B = (DEPTH + 2) // N_MIXERS
N_C = (DEPTH + 1) // N_MIXERS
N_D = DEPTH // N_MIXERS

kernel_name = "hybrid_bidir_interleaved_encoder"


def rmsnorm(x, g, eps=RMS_EPS):
    xf = x.astype(F32)
    y = xf * lax.rsqrt(jnp.mean(xf * xf, axis=-1, keepdims=True) + eps)
    return (y * g.astype(F32)).astype(x.dtype)


def layernorm(x, g, b, eps=LN_EPS):
    xf = x.astype(F32)
    mu = jnp.mean(xf, axis=-1, keepdims=True)
    var = jnp.mean(jnp.square(xf - mu), axis=-1, keepdims=True)
    y = (xf - mu) * lax.rsqrt(var + eps)
    return (y * g.astype(F32) + b.astype(F32)).astype(x.dtype)


def l2norm(x, eps=1e-6):
    xf = x.astype(F32)
    return xf * lax.rsqrt(jnp.sum(xf * xf, axis=-1, keepdims=True) + eps)


def dwconv_centred(x, w, b=None):
    width, ch = w.shape
    y = lax.conv_general_dilated(
        x, w[:, None, :].astype(x.dtype), window_strides=(1,),
        padding=[(width // 2, width // 2)],
        dimension_numbers=("NWC", "WIO", "NWC"), feature_group_count=ch)
    if b is not None:
        y = y + b.astype(x.dtype)
    return y


def rope_tables(T, dim):
    inv_freq = ROPE_THETA ** (-jnp.arange(0, dim, 2, dtype=F32) / dim)
    ang = jnp.arange(T, dtype=F32)[:, None] * inv_freq[None, :]
    return jnp.cos(ang), jnp.sin(ang)


def apply_rope(x, cos, sin):
    half = x.shape[-1] // 2
    xf = x.astype(F32)
    x1, x2 = xf[..., :half], xf[..., half:]
    c = cos[None, :, None, None, :]
    s = sin[None, :, None, None, :]
    return jnp.concatenate([x1 * c - x2 * s, x2 * c + x1 * s], axis=-1).astype(x.dtype)


def hyena_filters(T, w1, b1, w2, b2, w3, b3, w4, freq):
    t = jnp.linspace(0.0, 1.0, T, dtype=F32)[:, None]
    w = 2.0 * math.pi * jnp.arange(T, dtype=F32)[:, None] / T
    bands = jnp.linspace(1e-4, HY_BANDS - 1, HY_BANDS, dtype=F32)[None, :]
    z = jnp.concatenate([t, jnp.cos(bands * w), -jnp.sin(bands * w)], axis=-1)
    fr = freq.astype(F32)
    hdn = jnp.sin(fr * (z @ w1.astype(F32) + b1.astype(F32)))
    hdn = jnp.sin(fr * (hdn @ w2.astype(F32) + b2.astype(F32)))
    hdn = jnp.sin(fr * (hdn @ w3.astype(F32) + b3.astype(F32)))
    filt = (hdn @ w4.astype(F32)).reshape(T, 2, D_INNER)
    max_decay = math.log(HY_DECAY_TARGET) / HY_SHORT_DECAY_PCT
    min_decay = math.log(HY_DECAY_TARGET) / HY_LONG_DECAY_PCT
    deltas = jnp.abs(jnp.linspace(min_decay, max_decay, D_INNER, dtype=F32))
    filt = filt * jnp.exp(-t * deltas[None, :])[:, None, :]
    filt = filt / (jnp.sum(jnp.abs(filt), axis=(0, 1), keepdims=True) + HY_FILTER_EPS)
    return filt[:, 0], filt[:, 1]


def bidir_long_conv(v, h_fwd, h_bwd, skip):
    T = v.shape[1]
    n = 2 * T
    taps = jnp.concatenate(
        [h_fwd, jnp.zeros((1, h_fwd.shape[1]), F32), h_bwd[:0:-1]], axis=0)
    vf = v.astype(F32)
    y = jnp.fft.irfft(jnp.fft.rfft(vf, n=n, axis=1) * jnp.fft.rfft(taps, n=n, axis=0)[None],
                      n=n, axis=1)[:, :T]
    return (y + vf * skip.astype(F32)).astype(v.dtype)


def hyena_mixer(h, w_in, b_in, conv_w, conv_b, f_w1, f_b1, f_w2, f_b2, f_w3, f_b3, f_w4,
                f_freq, skip, w_out):
    T = h.shape[1]
    u = h @ w_in + b_in
    streams = dwconv_centred(u[..., :3 * D_INNER], conv_w, conv_b)
    x0, x1, v = jnp.split(streams, 3, axis=-1)
    z = u[..., 3 * D_INNER:]
    h_fwd, h_bwd = hyena_filters(T, f_w1, f_b1, f_w2, f_b2, f_w3, f_b3, f_w4, f_freq)
    v = bidir_long_conv(v * x1, h_fwd, h_bwd, skip)
    y = v * x0
    return (y * jax.nn.silu(z)) @ w_out


def diff_attention_mixer(h, w_in, lam_vecs, subln, w_out, cos, sin, layer_idx):
    B, T, _ = h.shape
    u = h @ w_in
    q = u[..., :DA_QK].reshape(B, T, DA_HEADS, 2, DA_HEAD_DIM)
    k = u[..., DA_QK:2 * DA_QK].reshape(B, T, DA_HEADS, 2, DA_HEAD_DIM)
    v = u[..., 2 * DA_QK:2 * DA_QK + D_INNER].reshape(B, T, DA_HEADS, DA_V_DIM)
    z = u[..., 2 * DA_QK + D_INNER:]
    q = apply_rope(q, cos, sin)
    k = apply_rope(k, cos, sin)
    lam_init = 0.8 - 0.6 * math.exp(-0.3 * layer_idx)
    lv = lam_vecs.astype(F32)
    lam = jnp.exp(jnp.sum(lv[0] * lv[1])) - jnp.exp(jnp.sum(lv[2] * lv[3])) + lam_init
    scale = DA_HEAD_DIM ** -0.5
    n_blk = -(-T // Q_BLOCK)
    q = jnp.pad(q, ((0, 0), (0, n_blk * Q_BLOCK - T), (0, 0), (0, 0), (0, 0)))
    q_blocks = q.reshape(B, n_blk, Q_BLOCK, DA_HEADS, 2, DA_HEAD_DIM).swapaxes(0, 1)

    def attend(qb):
        s = jnp.einsum("bqhmd,bkhmd->bhmqk", qb, k).astype(F32) * scale
        p = jax.nn.softmax(s, axis=-1)
        p = p[:, :, 0] - lam * p[:, :, 1]
        return jnp.einsum("bhqk,bkhd->bqhd", p.astype(v.dtype), v)

    o = lax.map(attend, q_blocks)
    o = o.swapaxes(0, 1).reshape(B, n_blk * Q_BLOCK, DA_HEADS, DA_V_DIM)[:, :T]
    o = rmsnorm(o, subln, DA_NORM_EPS) * (1.0 - lam_init)
    return (o.reshape(B, T, D_INNER) * jax.nn.silu(z)) @ w_out


def chunk_gated_delta(q, k, v, g, beta):
    B, Tp, H, dk = q.shape
    dv = v.shape[-1]
    C = GDN_CHUNK
    n = Tp // C

    def to_chunks(a):
        return a.reshape(B, n, C, H, -1).transpose(0, 3, 1, 2, 4)

    q, k, v = to_chunks(q), to_chunks(k), to_chunks(v)
    beta = to_chunks(beta[..., None])
    g = jnp.cumsum(to_chunks(g[..., None])[..., 0], axis=-1)
    idx = jnp.arange(C)
    strict = idx[:, None] > idx[None, :]
    incl = idx[:, None] >= idx[None, :]
    gdiff = g[..., :, None] - g[..., None, :]
    k_beta = k * beta
    a_mat = jnp.einsum("bhncd,bhnsd->bhncs", k_beta, k) * jnp.exp(jnp.where(strict, gdiff, -jnp.inf))
    rhs = jnp.concatenate([v * beta, k_beta * jnp.exp(g)[..., None]], axis=-1)
    sol = lax.linalg.triangular_solve(a_mat + jnp.eye(C, dtype=F32), rhs,
                                      left_side=True, lower=True)
    u_vals, w_vals = sol[..., :dv], sol[..., dv:]
    attn_intra = jnp.einsum("bhncd,bhnsd->bhncs", q, k) * jnp.exp(jnp.where(incl, gdiff, -jnp.inf))
    q_dec = q * jnp.exp(g)[..., None]
    g_last = g[..., -1]
    k_end = k * jnp.exp(g_last[..., None] - g)[..., None]

    def step(S, xs):
        qd, ke, uc, wc, ac, gl = xs
        v_new = uc - jnp.einsum("bhcd,bhde->bhce", wc, S)
        o = jnp.einsum("bhcd,bhde->bhce", qd, S) + jnp.einsum("bhcs,bhse->bhce", ac, v_new)
        S = S * jnp.exp(gl)[..., None, None] + jnp.einsum("bhcd,bhce->bhde", ke, v_new)
        return S, o

    xs = tuple(jnp.moveaxis(a, 2, 0) for a in (q_dec, k_end, u_vals, w_vals, attn_intra, g_last))
    S0 = jnp.zeros((B, H, dk, dv), F32)
    _, o = lax.scan(step, S0, xs)
    return o.transpose(1, 0, 3, 2, 4).reshape(B, Tp, H, dv)


def gdn_mixer(h, w_in, conv_w, a_log, dt_bias, o_norm, w_out):
    B, T, _ = h.shape
    u = h @ w_in
    n_qkv = 2 * GDN_QK + D_INNER
    qkv = jax.nn.silu(dwconv_centred(u[..., :n_qkv], conv_w))
    z = u[..., n_qkv:n_qkv + D_INNER].reshape(B, T, GDN_V_HEADS, GDN_HEAD_DIM)
    ab = u[..., n_qkv + D_INNER:].astype(F32).reshape(B, T, 2, 2, GDN_V_HEADS)
    q = l2norm(qkv[..., :GDN_QK].reshape(B, T, GDN_K_HEADS, GDN_HEAD_DIM)) * GDN_HEAD_DIM ** -0.5
    k = l2norm(qkv[..., GDN_QK:2 * GDN_QK].reshape(B, T, GDN_K_HEADS, GDN_HEAD_DIM))
    v = qkv[..., 2 * GDN_QK:].astype(F32).reshape(B, T, GDN_V_HEADS, GDN_HEAD_DIM)
    rep = GDN_V_HEADS // GDN_K_HEADS
    q = jnp.repeat(q, rep, axis=2)
    k = jnp.repeat(k, rep, axis=2)
    g = -jnp.exp(a_log.astype(F32)) * jax.nn.softplus(ab[:, :, 0] + dt_bias.astype(F32))
    beta = jax.nn.sigmoid(ab[:, :, 1])
    pad_front = (-N_META) % GDN_CHUNK
    pad_end = (-(T - N_META)) % GDN_CHUNK

    def padt(a):
        return jnp.pad(a, [(0, 0), (pad_front, pad_end)] + [(0, 0)] * (a.ndim - 2))

    q, k, v, g, beta = padt(q), padt(k), padt(v), padt(g), padt(beta)

    def rev(a):
        return jnp.flip(a, axis=1)

    o_fwd = chunk_gated_delta(q, k, v, g[:, :, 0], beta[:, :, 0])
    o_bwd = rev(chunk_gated_delta(rev(q), rev(k), rev(v), rev(g[:, :, 1]), rev(beta[:, :, 1])))
    o = (o_fwd + o_bwd)[:, pad_front:pad_front + T].astype(h.dtype)
    o = rmsnorm(o, o_norm) * jax.nn.silu(z)
    return o.reshape(B, T, D_INNER) @ w_out


def conformer_conv_mixer(h, w_in, b_in, dw_w, dw_b, ln_g, ln_b, w_out, b_out):
    u = h @ w_in + b_in
    a, a_gate, z = jnp.split(u, 3, axis=-1)
    y = a * jax.nn.sigmoid(a_gate)
    y = dwconv_centred(y, dw_w, dw_b)
    y = jax.nn.silu(layernorm(y, ln_g, ln_b))
    return (y * jax.nn.silu(z)) @ w_out + b_out


def setup_inputs(seed: int = 0) -> dict:
    key = jax.random.key(seed)
    ks = jax.random.split(key, 48)
    counter = [0]

    def nk():
        counter[0] += 1
        return ks[counter[0] - 1]

    def nrm(shape, scale):
        return jax.random.normal(nk(), shape, F32) * scale

    def gain(shape):
        return 1.0 + nrm(shape, 0.02)

    E = D_INNER
    dt = jnp.exp(jax.random.uniform(nk(), (N_C, 2, GDN_V_HEADS), F32, math.log(1e-3), math.log(1e-1)))
    a_log = jnp.log(jax.random.uniform(nk(), (N_C, 2, GDN_V_HEADS), F32, 1.0, 16.0))
    return {
        "x": nrm((BATCH, SEQ, D_MODEL), 1.0),
        "meta": nrm((N_META, D_MODEL), 1.0),
        "norm_pre": gain((DEPTH, D_MODEL)),
        "norm_post": gain((DEPTH, D_MODEL)),
        "hy_w_in": nrm((N_A, D_MODEL, 4 * E), D_MODEL ** -0.5),
        "hy_b_in": nrm((N_A, 4 * E), 0.02),
        "hy_conv_w": nrm((N_A, HY_SHORT, 3 * E), HY_SHORT ** -0.5),
        "hy_conv_b": nrm((N_A, 3 * E), 0.02),
        "hy_f_w1": nrm((N_A, HY_EMB, HY_HIDDEN), HY_EMB ** -0.5),
        "hy_f_b1": nrm((N_A, HY_HIDDEN), 0.02),
        "hy_f_w2": nrm((N_A, HY_HIDDEN, HY_HIDDEN), HY_HIDDEN ** -0.5),
        "hy_f_b2": nrm((N_A, HY_HIDDEN), 0.02),
        "hy_f_w3": nrm((N_A, HY_HIDDEN, HY_HIDDEN), HY_HIDDEN ** -0.5),
        "hy_f_b3": nrm((N_A, HY_HIDDEN), 0.02),
        "hy_f_w4": nrm((N_A, HY_HIDDEN, 2 * E), HY_HIDDEN ** -0.5),
        "hy_f_freq": gain((N_A, HY_HIDDEN)),
        "hy_skip": nrm((N_A, E), 1.0),
        "hy_w_out": nrm((N_A, E, D_MODEL), E ** -0.5),
        "da_w_in": nrm((N_B, D_MODEL, 2 * DA_QK + 2 * E), D_MODEL ** -0.5),
        "da_lambda": nrm((N_B, 4, DA_HEAD_DIM), 0.1),
        "da_subln": gain((N_B, DA_V_DIM)),
        "da_w_out": nrm((N_B, E, D_MODEL), E ** -0.5),
        "gdn_w_in": nrm((N_C, D_MODEL, 2 * GDN_QK + 2 * E + 4 * GDN_V_HEADS), D_MODEL ** -0.5),
        "gdn_conv_w": nrm((N_C, GDN_SHORT, 2 * GDN_QK + E), GDN_SHORT ** -0.5),
        "gdn_a_log": a_log,
        "gdn_dt_bias": dt + jnp.log(-jnp.expm1(-dt)),
        "gdn_o_norm": gain((N_C, GDN_HEAD_DIM)),
        "gdn_w_out": nrm((N_C, E, D_MODEL), E ** -0.5),
        "cf_w_in": nrm((N_D, D_MODEL, 3 * E), D_MODEL ** -0.5),
        "cf_b_in": nrm((N_D, 3 * E), 0.02),
        "cf_dw_w": nrm((N_D, CF_WIDTH, E), CF_WIDTH ** -0.5),
        "cf_dw_b": nrm((N_D, E), 0.02),
        "cf_ln_g": gain((N_D, E)),
        "cf_ln_b": nrm((N_D, E), 0.02),
        "cf_w_out": nrm((N_D, E, D_MODEL), E ** -0.5),
        "cf_b_out": nrm((N_D, D_MODEL), 0.02),
    }


def reference(x, meta, norm_pre, norm_post,
              hy_w_in, hy_b_in, hy_conv_w, hy_conv_b, hy_f_w1, hy_f_b1, hy_f_w2, hy_f_b2,
              hy_f_w3, hy_f_b3, hy_f_w4, hy_f_freq, hy_skip, hy_w_out,
              da_w_in, da_lambda, da_subln, da_w_out,
              gdn_w_in, gdn_conv_w, gdn_a_log, gdn_dt_bias, gdn_o_norm, gdn_w_out,
              cf_w_in, cf_b_in, cf_dw_w, cf_dw_b, cf_ln_g, cf_ln_b, cf_w_out, cf_b_out):
    B = x.shape[0]
    h = jnp.concatenate(
        [jnp.broadcast_to(meta[None].astype(x.dtype), (B, N_META, D_MODEL)), x], axis=1)
    T = h.shape[1]
    cos, sin = rope_tables(T, DA_HEAD_DIM)
    for i in range(DEPTH):
        m, j = i % N_MIXERS, i // N_MIXERS
        y = rmsnorm(h, norm_pre[i])
        if m == 0:
            y = hyena_mixer(y, hy_w_in[j], hy_b_in[j], hy_conv_w[j], hy_conv_b[j],
                            hy_f_w1[j], hy_f_b1[j], hy_f_w2[j], hy_f_b2[j], hy_f_w3[j], hy_f_b3[j],
                            hy_f_w4[j], hy_f_freq[j], hy_skip[j], hy_w_out[j])
        elif m == 1:
            y = diff_attention_mixer(y, da_w_in[j], da_lambda[j], da_subln[j], da_w_out[j],
                                     cos, sin, i)
        elif m == 2:
            y = gdn_mixer(y, gdn_w_in[j], gdn_conv_w[j], gdn_a_log[j], gdn_dt_bias[j],
                          gdn_o_norm[j], gdn_w_out[j])
        else:
            y = conformer_conv_mixer(y, cf_w_in[j], cf_b_in[j], cf_dw_w[j], cf_dw_b[j],
                                     cf_ln_g[j], cf_ln_b[j], cf_w_out[j], cf_b_out[j])
        h = h + rmsnorm(y, norm_post[i])
    return h[:, N_META:]
```

```python
import functools
import math

import numpy as np
import jax
import jax.numpy as jnp
from jax import lax
from jax.experimental import pallas as pl
from jax.experimental.pallas import tpu as pltpu

F32 = jnp.float32
BF16 = jnp.bfloat16

N_META = 16
FRONT = 112
HALO = 16
LANE = 128
RMS_EPS = 1e-6
LN_EPS = 1e-5
ROPE_THETA = 10000.0
VMEM_LIMIT = 56 * 1024 * 1024

HY_SHORT = 3
HY_BANDS = 16
HY_HIDDEN = 64
HY_SHORT_DECAY_PCT = 0.3
HY_LONG_DECAY_PCT = 1.5
HY_DECAY_TARGET = 1e-2
HY_FILTER_EPS = 1e-6
DA_HEAD_DIM = 64
DA_V_DIM = 128
DA_NORM_EPS = 1e-5
GDN_HEAD_DIM = 128
GDN_CHUNK = 128
CF_WIDTH = 31

_NT = (((1,), (1,)), ((), ()))


def _tile(n, target, mult=8):
    best = None
    for t in range(mult, min(n, target) + 1, mult):
        if n % t == 0:
            best = t
    assert best is not None, (n, target, mult)
    return best


def _params(*sem):
    return pltpu.CompilerParams(dimension_semantics=sem, vmem_limit_bytes=VMEM_LIMIT)


def _silu(x):
    return x * jax.nn.sigmoid(x)


def _row_valid(base, n, Tp):
    rows = base + lax.broadcasted_iota(jnp.int32, (n, 1), 0)
    return (rows >= FRONT) & (rows < Tp)


def _in_proj_kernel(h_ref, g_ref, w_ref, b_ref, o_ref):
    x = h_ref[0]
    ms = jnp.mean(x * x, axis=-1, keepdims=True)
    y = (x * lax.rsqrt(ms + RMS_EPS) * g_ref[...]).astype(BF16)
    acc = jnp.dot(y, w_ref[...], preferred_element_type=F32)
    o_ref[0] = (acc + b_ref[...]).astype(o_ref.dtype)


def _in_proj(h, gain, w, bias):
    B, Tp, D = h.shape
    N = w.shape[1]
    tm = _tile(Tp, 1056)
    tn = _tile(N, 2048, LANE)
    return pl.pallas_call(
        _in_proj_kernel,
        grid=(N // tn, B, Tp // tm),
        in_specs=[
            pl.BlockSpec((1, tm, D), lambda n, b, i: (b, i, 0)),
            pl.BlockSpec((1, D), lambda n, b, i: (0, 0)),
            pl.BlockSpec((D, tn), lambda n, b, i: (0, n)),
            pl.BlockSpec((1, tn), lambda n, b, i: (0, n)),
        ],
        out_specs=pl.BlockSpec((1, tm, tn), lambda n, b, i: (b, i, n)),
        out_shape=jax.ShapeDtypeStruct((B, Tp, N), BF16),
        compiler_params=_params("parallel", "parallel", "parallel"),
        name="in_proj",
    )(h, gain.reshape(1, D).astype(F32), w, bias.reshape(1, N).astype(F32))


def _post_norm_residual(y, g_ref, h_ref, o_ref):
    ms = jnp.mean(y * y, axis=-1, keepdims=True)
    o_ref[0] = h_ref[0] + y * lax.rsqrt(ms + RMS_EPS) * g_ref[...]


def _out_proj_kernel(a_ref, w_ref, b_ref, g_ref, h_ref, o_ref):
    y = jnp.dot(a_ref[0], w_ref[...], preferred_element_type=F32) + b_ref[...]
    _post_norm_residual(y, g_ref, h_ref, o_ref)


def _out_proj(a, w, bias, gain, h):
    B, Tp, D = h.shape
    E = a.shape[2]
    tm = _tile(Tp, 528)
    return pl.pallas_call(
        _out_proj_kernel,
        grid=(B, Tp // tm),
        in_specs=[
            pl.BlockSpec((1, tm, E), lambda b, i: (b, i, 0)),
            pl.BlockSpec((E, D), lambda b, i: (0, 0)),
            pl.BlockSpec((1, D), lambda b, i: (0, 0)),
            pl.BlockSpec((1, D), lambda b, i: (0, 0)),
            pl.BlockSpec((1, tm, D), lambda b, i: (b, i, 0)),
        ],
        out_specs=pl.BlockSpec((1, tm, D), lambda b, i: (b, i, 0)),
        out_shape=jax.ShapeDtypeStruct((B, Tp, D), F32),
        compiler_params=_params("parallel", "parallel"),
        name="out_proj",
    )(a, w, bias.reshape(1, D).astype(F32), gain.reshape(1, D).astype(F32), h)


def _halo_specs(width, tt, Tp, col):
    nt, nh, r = Tp // tt, Tp // HALO, tt // HALO
    main = pl.BlockSpec((1, tt, width), lambda b, i, *_: (b, jnp.minimum(i, nt - 1), col(*_)))
    left = pl.BlockSpec((1, HALO, width),
                        lambda b, i, *_: (b, jnp.clip(i * r - 1, 0, nh - 1), col(*_)))
    right = pl.BlockSpec((1, HALO, width),
                         lambda b, i, *_: (b, jnp.clip((i + 1) * r, 0, nh - 1), col(*_)))
    return main, left, right


CF_ROWS = 64


def _cf_kernel(am, al, ar, gm, gl, gr, z_ref, w_ref, b_ref, lg_ref, lb_ref, o_ref, G_ref, Y_ref,
               *, tt, Tp, E):
    row0 = pl.program_id(1) * tt

    def glu(a_ref, g_ref, base, n):
        a = a_ref[0].astype(F32)
        g = g_ref[0].astype(F32)
        return jnp.where(_row_valid(base, n, Tp), a * jax.nn.sigmoid(g), 0.0)

    G_ref[0:HALO] = glu(al, gl, row0 - HALO, HALO)
    G_ref[HALO:HALO + tt] = glu(am, gm, row0, tt)
    G_ref[HALO + tt:HALO + tt + HALO] = glu(ar, gr, row0 + tt, HALO)

    shift = HALO - CF_WIDTH // 2

    def lane_body(c, carry):
        off = pl.multiple_of(c * LANE, LANE)
        lanes = pl.ds(off, LANE)
        bias = b_ref[:, lanes]
        for rb in range(tt // CF_ROWS):
            acc = jnp.zeros((CF_ROWS, LANE), F32)
            for k in range(CF_WIDTH):
                acc = acc + w_ref[k:k + 1, lanes] * G_ref[pl.ds(rb * CF_ROWS + k + shift, CF_ROWS), lanes]
            Y_ref[pl.ds(rb * CF_ROWS, CF_ROWS), lanes] = acc + bias
        return carry

    lax.fori_loop(0, E // LANE, lane_body, 0)

    y = Y_ref[...]
    mu = jnp.mean(y, axis=-1, keepdims=True)
    yc = y - mu
    var = jnp.mean(yc * yc, axis=-1, keepdims=True)
    yn = yc * lax.rsqrt(var + LN_EPS) * lg_ref[...] + lb_ref[...]
    o_ref[0] = (_silu(yn) * _silu(z_ref[0].astype(F32))).astype(o_ref.dtype)


def _conformer_core(u, dw_w, dw_b, ln_g, ln_b):
    B, Tp, E3 = u.shape
    E = E3 // 3
    tt = _tile(Tp, 384, CF_ROWS)
    a_specs = _halo_specs(E, tt, Tp, lambda: 0)
    g_specs = _halo_specs(E, tt, Tp, lambda: 1)
    vec = pl.BlockSpec((1, E), lambda b, i: (0, 0))
    return pl.pallas_call(
        functools.partial(_cf_kernel, tt=tt, Tp=Tp, E=E),
        grid=(B, Tp // tt),
        in_specs=[*a_specs, *g_specs,
                  pl.BlockSpec((1, tt, E), lambda b, i: (b, i, 2)),
                  pl.BlockSpec((CF_WIDTH, E), lambda b, i: (0, 0)),
                  vec, vec, vec],
        out_specs=pl.BlockSpec((1, tt, E), lambda b, i: (b, i, 0)),
        out_shape=jax.ShapeDtypeStruct((B, Tp, E), BF16),
        scratch_shapes=[pltpu.VMEM((tt + 2 * HALO, E), F32), pltpu.VMEM((tt, E), F32)],
        compiler_params=_params("parallel", "parallel"),
        name="conformer_conv",
    )(u, u, u, u, u, u, u, dw_w.astype(F32), dw_b.reshape(1, E).astype(F32),
      ln_g.reshape(1, E).astype(F32), ln_b.reshape(1, E).astype(F32))


def _rope_kernel(u_ref, c_ref, s_ref, o_ref, *, n_q, scale):
    cos = c_ref[...]
    sin = s_ref[...]
    lane = lax.broadcasted_iota(jnp.int32, (1, LANE), 1)
    first = (lane % DA_HEAD_DIM) < DA_HEAD_DIM // 2
    for g in range(u_ref.shape[2] // LANE):
        x = u_ref[0, :, g * LANE:(g + 1) * LANE].astype(F32)
        partner = jnp.where(first, pltpu.roll(x, LANE - DA_HEAD_DIM // 2, axis=1),
                            pltpu.roll(x, DA_HEAD_DIM // 2, axis=1))
        r = x * cos + partner * sin
        if g < n_q:
            r = r * scale
        o_ref[0, :, g * LANE:(g + 1) * LANE] = r.astype(o_ref.dtype)


def _rope(u, n_qk, Tp):
    B = u.shape[0]
    half = DA_HEAD_DIM // 2
    inv_freq = ROPE_THETA ** (-jnp.arange(0, DA_HEAD_DIM, 2, dtype=F32) / DA_HEAD_DIM)
    pos = (jnp.arange(Tp, dtype=jnp.int32) - FRONT).astype(F32)
    ang = pos[:, None] * inv_freq[None, :]
    cos = jnp.tile(jnp.cos(ang), (1, LANE // half))
    sin = jnp.tile(jnp.concatenate([-jnp.sin(ang), jnp.sin(ang)], axis=1), (1, LANE // DA_HEAD_DIM))
    tm = _tile(Tp, 384)
    return pl.pallas_call(
        functools.partial(_rope_kernel, n_q=n_qk // LANE, scale=DA_HEAD_DIM ** -0.5),
        grid=(B, Tp // tm),
        in_specs=[pl.BlockSpec((1, tm, 2 * n_qk), lambda b, i: (b, i, 0)),
                  pl.BlockSpec((tm, LANE), lambda b, i: (i, 0)),
                  pl.BlockSpec((tm, LANE), lambda b, i: (i, 0))],
        out_specs=pl.BlockSpec((1, tm, 2 * n_qk), lambda b, i: (b, i, 0)),
        out_shape=jax.ShapeDtypeStruct((B, Tp, 2 * n_qk), BF16),
        compiler_params=_params("parallel", "parallel"),
        name="rope",
    )(u, cos, sin)


def _attn_kernel(lv_ref, sub_ref, q_ref, k_ref, v_ref, z_ref, o_ref, *, lam_init, Tp):
    q = q_ref[0]
    k = k_ref[0]
    v = v_ref[0]
    lane = lax.broadcasted_iota(jnp.int32, (1, LANE), 1)
    kvalid = lax.broadcasted_iota(jnp.int32, (1, Tp), 1) >= FRONT
    outs = []
    for m in range(2):
        sel = (lane < DA_HEAD_DIM) if m == 0 else (lane >= DA_HEAD_DIM)
        qm = jnp.where(sel, q, jnp.zeros_like(q))
        s = lax.dot_general(qm, k, _NT, preferred_element_type=F32)
        s = jnp.where(kvalid, s, -1e30)
        mx = jnp.max(s, axis=-1, keepdims=True)
        p = jnp.exp(s - mx)
        l = jnp.sum(p, axis=-1, keepdims=True)
        outs.append(jnp.dot(p.astype(BF16), v, preferred_element_type=F32) / l)
    lv = lv_ref[...]
    lam = (jnp.exp(jnp.sum(lv[0:1] * lv[1:2], axis=-1, keepdims=True))
           - jnp.exp(jnp.sum(lv[2:3] * lv[3:4], axis=-1, keepdims=True)) + lam_init)
    o = outs[0] - lam * outs[1]
    ms = jnp.mean(o * o, axis=-1, keepdims=True)
    o = o * lax.rsqrt(ms + DA_NORM_EPS) * sub_ref[...] * (1.0 - lam_init)
    o_ref[0] = (o * _silu(z_ref[0].astype(F32))).astype(o_ref.dtype)


def _diff_attention_core(u, qk, lam_vecs, subln, layer_idx, E):
    B, Tp, _ = u.shape
    H = E // DA_V_DIM
    tq = _tile(Tp, 384)
    lam_init = 0.8 - 0.6 * math.exp(-0.3 * layer_idx)
    return pl.pallas_call(
        functools.partial(_attn_kernel, lam_init=lam_init, Tp=Tp),
        grid=(B, H, Tp // tq),
        in_specs=[
            pl.BlockSpec((4, DA_HEAD_DIM), lambda b, h, i: (0, 0)),
            pl.BlockSpec((1, DA_V_DIM), lambda b, h, i: (0, 0)),
            pl.BlockSpec((1, tq, LANE), lambda b, h, i: (b, i, h)),
            pl.BlockSpec((1, Tp, LANE), lambda b, h, i: (b, 0, H + h)),
            pl.BlockSpec((1, Tp, LANE), lambda b, h, i: (b, 0, 2 * H + h)),
            pl.BlockSpec((1, tq, LANE), lambda b, h, i: (b, i, 3 * H + h)),
        ],
        out_specs=pl.BlockSpec((1, tq, LANE), lambda b, h, i: (b, i, h)),
        out_shape=jax.ShapeDtypeStruct((B, Tp, E), BF16),
        compiler_params=_params("parallel", "parallel", "parallel"),
        name="diff_attention",
    )(lam_vecs.astype(F32), subln.reshape(1, DA_V_DIM).astype(F32), qk, qk, u, u)


def _gdn_prep_kernel(um, ul, ur, ab_ref, cw_ref, al_ref, dt_ref,
                     q_ref, k_ref, kt_ref, v_ref, gb_ref, gbt_ref, U_ref, *, tt, Tp, n_qk, n_g):
    row0 = pl.program_id(1) * tt

    def masked(ref, base, n):
        return jnp.where(_row_valid(base, n, Tp), ref[0].astype(F32), 0.0)

    U_ref[0:HALO] = masked(ul, row0 - HALO, HALO)
    U_ref[HALO:HALO + tt] = masked(um, row0, tt)
    U_ref[HALO + tt:HALO + tt + HALO] = masked(ur, row0 + tt, HALO)
    valid = _row_valid(row0, tt, Tp)

    def conv_silu(c0, width):
        cols = slice(c0, c0 + width)
        y = (cw_ref[0:1, cols] * U_ref[HALO - 1:HALO - 1 + tt, cols]
             + cw_ref[1:2, cols] * U_ref[HALO:HALO + tt, cols]
             + cw_ref[2:3, cols] * U_ref[HALO + 1:HALO + 1 + tt, cols])
        return _silu(y)

    def l2n(x):
        return x * lax.rsqrt(jnp.sum(x * x, axis=-1, keepdims=True) + 1e-6)

    for hh in range(n_qk // LANE):
        c0 = hh * LANE
        qh = l2n(conv_silu(c0, LANE)) * GDN_HEAD_DIM ** -0.5
        q_ref[0, :, c0:c0 + LANE] = jnp.where(valid, qh, 0.0).astype(q_ref.dtype)
        kh = jnp.where(valid, l2n(conv_silu(n_qk + c0, LANE)), 0.0)
        k_ref[0, :, c0:c0 + LANE] = kh.astype(k_ref.dtype)
        kt_ref[0, c0:c0 + LANE, :] = kh.T.astype(kt_ref.dtype)
    n_v = v_ref.shape[2]
    for c0 in range(0, n_v, LANE):
        vh = conv_silu(2 * n_qk + c0, LANE)
        v_ref[0, :, c0:c0 + LANE] = jnp.where(valid, vh, 0.0).astype(v_ref.dtype)

    x = ab_ref[0].astype(F32)
    lane = lax.broadcasted_iota(jnp.int32, (1, LANE), 1)
    xa = x + dt_ref[...]
    softplus = jnp.maximum(xa, 0.0) + jnp.log(1.0 + jnp.exp(-jnp.abs(xa)))
    g = -jnp.exp(al_ref[...]) * softplus
    beta = jax.nn.sigmoid(x)
    gb = jnp.where(lane < n_g, g, jnp.where(lane < 2 * n_g, beta, 0.0))
    gb = jnp.where(valid, gb, 0.0)
    gb_ref[0] = gb
    gbt_ref[0] = gb.T


def _gdn_prep(u, conv_w, a_log, dt_bias, n_qk, E):
    B, Tp, _ = u.shape
    n_conv = 2 * n_qk + E
    n_g = a_log.size
    tt = _tile(Tp, 384, LANE)
    specs = _halo_specs(n_conv, tt, Tp, lambda: 0)
    al = jnp.zeros((1, LANE), F32).at[0, :n_g].set(a_log.reshape(-1).astype(F32))
    dt = jnp.zeros((1, LANE), F32).at[0, :n_g].set(dt_bias.reshape(-1).astype(F32))
    vec = pl.BlockSpec((1, LANE), lambda b, i: (0, 0))
    return pl.pallas_call(
        functools.partial(_gdn_prep_kernel, tt=tt, Tp=Tp, n_qk=n_qk, n_g=n_g),
        grid=(B, Tp // tt),
        in_specs=[*specs,
                  pl.BlockSpec((1, tt, LANE), lambda b, i: (b, i, (n_conv + E) // LANE)),
                  pl.BlockSpec((HY_SHORT, n_conv), lambda b, i: (0, 0)),
                  vec, vec],
        out_specs=[
            pl.BlockSpec((1, tt, n_qk), lambda b, i: (b, i, 0)),
            pl.BlockSpec((1, tt, n_qk), lambda b, i: (b, i, 0)),
            pl.BlockSpec((1, n_qk, tt), lambda b, i: (b, 0, i)),
            pl.BlockSpec((1, tt, E), lambda b, i: (b, i, 0)),
            pl.BlockSpec((1, tt, LANE), lambda b, i: (b, i, 0)),
            pl.BlockSpec((1, LANE, tt), lambda b, i: (b, 0, i)),
        ],
        out_shape=[
            jax.ShapeDtypeStruct((B, Tp, n_qk), BF16),
            jax.ShapeDtypeStruct((B, Tp, n_qk), BF16),
            jax.ShapeDtypeStruct((B, n_qk, Tp), BF16),
            jax.ShapeDtypeStruct((B, Tp, E), BF16),
            jax.ShapeDtypeStruct((B, Tp, LANE), F32),
            jax.ShapeDtypeStruct((B, LANE, Tp), F32),
        ],
        scratch_shapes=[pltpu.VMEM((tt + 2 * HALO, n_conv), F32)],
        compiler_params=_params("parallel", "parallel"),
        name="gdn_prep",
    )(u, u, u, u, conv_w.astype(F32), al, dt)


def _unit_lower_inverse(a):
    C = a.shape[0]
    row = lax.broadcasted_iota(jnp.int32, (C, C), 0)
    col = lax.broadcasted_iota(jnp.int32, (C, C), 1)
    rc = row ^ col
    p = jnp.where(rc < 8, a, 0.0)
    x = jnp.where(rc == 0, 1.0, 0.0) - p
    n = 2
    while n < 8:
        pb = p.astype(BF16)
        p = jnp.dot(pb, pb, preferred_element_type=F32)
        x = x + jnp.dot(x.astype(BF16), p.astype(BF16), preferred_element_type=F32)
        n *= 2
    b = 8
    while b < C:
        off = jnp.where((rc >= b) & (rc < 2 * b), a, 0.0).astype(BF16)
        xb = x.astype(BF16)
        x = x - jnp.dot(jnp.dot(xb, off, preferred_element_type=F32).astype(BF16), xb,
                        preferred_element_type=F32)
        b *= 2
    return x


def _gdn_chunk_kernel(q_ref, k_ref, kt_ref, v_ref, gb_ref, gbt_ref, o_ref, S_ref, *, n_heads):
    d = pl.program_id(0)
    kh = pl.program_id(2)

    @pl.when(pl.program_id(3) == 0)
    def _():
        S_ref[...] = jnp.zeros_like(S_ref)

    C = q_ref.shape[1]
    q = q_ref[0]
    k = k_ref[0]
    kt = kt_ref[0].astype(F32)
    kf = k.astype(F32)
    qf = q.astype(F32)
    kk = lax.dot_general(k, k, _NT, preferred_element_type=F32)
    qk = lax.dot_general(q, k, _NT, preferred_element_type=F32)

    row = lax.broadcasted_iota(jnp.int32, (C, C), 0)
    col = lax.broadcasted_iota(jnp.int32, (C, C), 1)
    ahead = (row - col) * (1 - 2 * d)
    incl = ahead >= 0
    strict = ahead > 0
    incl_t = ahead <= 0
    gb = gb_ref[0]
    hi = lax.Precision.HIGHEST
    gc_all = jnp.dot(incl.astype(F32), gb, precision=hi, preferred_element_type=F32)
    gct_all = jnp.dot(gbt_ref[0], incl_t.astype(F32), precision=hi, preferred_element_type=F32)
    lane = lax.broadcasted_iota(jnp.int32, (1, LANE), 1)
    sub = lax.broadcasted_iota(jnp.int32, (LANE, 1), 0)

    for j in range(2):
        idx = d * n_heads + 2 * kh + j
        gc = jnp.sum(jnp.where(lane == idx, gc_all, 0.0), axis=1, keepdims=True)
        beta = jnp.sum(jnp.where(lane == 2 * n_heads + idx, gb, 0.0), axis=1, keepdims=True)
        gc_row = jnp.sum(jnp.where(sub == idx, gct_all, 0.0), axis=0, keepdims=True)
        gtot = jnp.sum(jnp.where(lane == idx, gb, 0.0), keepdims=True)
        gdiff = gc - gc_row
        a = beta * kk * jnp.exp(jnp.where(strict, gdiff, -1e30))
        attn = qk * jnp.exp(jnp.where(incl, gdiff, -1e30))
        t_inv = _unit_lower_inverse(a)
        e_gc = jnp.exp(gc)
        vj = v_ref[0, :, j * LANE:(j + 1) * LANE].astype(F32)
        rhs = jnp.concatenate([vj * beta, kf * (beta * e_gc)], axis=1).astype(BF16)
        sol = jnp.dot(t_inv.astype(BF16), rhs, preferred_element_type=F32)
        u_val, w_val = sol[:, :LANE], sol[:, LANE:]
        s_old = S_ref[j]
        lhs = jnp.concatenate([w_val, qf * e_gc], axis=0).astype(BF16)
        r = jnp.dot(lhs, s_old.astype(BF16), preferred_element_type=F32)
        v_new = (u_val - r[:C]).astype(BF16)
        o = r[C:] + jnp.dot(attn.astype(BF16), v_new, preferred_element_type=F32)
        k_end_t = (kt * jnp.exp(gtot - gc_row)).astype(BF16)
        S_ref[j] = s_old * jnp.exp(gtot) + jnp.dot(k_end_t, v_new, preferred_element_type=F32)
        o_ref[0, 0, :, j * LANE:(j + 1) * LANE] = o.astype(o_ref.dtype)


def _gdn_scan(q, k, kt, v, gb, gbt):
    B, Tp, n_qk = q.shape
    E = v.shape[2]
    Hk = n_qk // GDN_HEAD_DIM
    n_heads = E // GDN_HEAD_DIM
    C = GDN_CHUNK
    nC = Tp // C

    def cc(d, c):
        return c + d * (nC - 1 - 2 * c)

    return pl.pallas_call(
        functools.partial(_gdn_chunk_kernel, n_heads=n_heads),
        grid=(2, B, Hk, nC),
        in_specs=[
            pl.BlockSpec((1, C, LANE), lambda d, b, h, c: (b, cc(d, c), h)),
            pl.BlockSpec((1, C, LANE), lambda d, b, h, c: (b, cc(d, c), h)),
            pl.BlockSpec((1, LANE, C), lambda d, b, h, c: (b, h, cc(d, c))),
            pl.BlockSpec((1, C, 2 * LANE), lambda d, b, h, c: (b, cc(d, c), h)),
            pl.BlockSpec((1, C, LANE), lambda d, b, h, c: (b, cc(d, c), 0)),
            pl.BlockSpec((1, LANE, C), lambda d, b, h, c: (b, 0, cc(d, c))),
        ],
        out_specs=pl.BlockSpec((1, 1, C, 2 * LANE), lambda d, b, h, c: (d, b, cc(d, c), h)),
        out_shape=jax.ShapeDtypeStruct((2, B, Tp, E), BF16),
        scratch_shapes=[pltpu.VMEM((2, GDN_HEAD_DIM, GDN_HEAD_DIM), F32)],
        compiler_params=_params("parallel", "parallel", "parallel", "arbitrary"),
        name="gdn_scan",
    )(q, k, kt, v, gb, gbt)


def _gdn_out_kernel(of_ref, ob_ref, z_ref, on_ref, w_ref, g_ref, h_ref, o_ref):
    E = of_ref.shape[3]
    parts = []
    for c0 in range(0, E, LANE):
        o = of_ref[0, 0, :, c0:c0 + LANE].astype(F32) + ob_ref[0, 0, :, c0:c0 + LANE].astype(F32)
        ms = jnp.mean(o * o, axis=-1, keepdims=True)
        o = o * lax.rsqrt(ms + RMS_EPS) * on_ref[...]
        parts.append((o * _silu(z_ref[0, :, c0:c0 + LANE].astype(F32))).astype(BF16))
    a = jnp.concatenate(parts, axis=1)
    y = jnp.dot(a, w_ref[...], preferred_element_type=F32)
    _post_norm_residual(y, g_ref, h_ref, o_ref)


def _gdn_out(o2, u, z_col, o_norm, w, gain, h):
    B, Tp, D = h.shape
    E = o2.shape[3]
    tm = _tile(Tp, 528)
    return pl.pallas_call(
        _gdn_out_kernel,
        grid=(B, Tp // tm),
        in_specs=[
            pl.BlockSpec((1, 1, tm, E), lambda b, i: (0, b, i, 0)),
            pl.BlockSpec((1, 1, tm, E), lambda b, i: (1, b, i, 0)),
            pl.BlockSpec((1, tm, E), lambda b, i: (b, i, z_col)),
            pl.BlockSpec((1, GDN_HEAD_DIM), lambda b, i: (0, 0)),
            pl.BlockSpec((E, D), lambda b, i: (0, 0)),
            pl.BlockSpec((1, D), lambda b, i: (0, 0)),
            pl.BlockSpec((1, tm, D), lambda b, i: (b, i, 0)),
        ],
        out_specs=pl.BlockSpec((1, tm, D), lambda b, i: (b, i, 0)),
        out_shape=jax.ShapeDtypeStruct((B, Tp, D), F32),
        compiler_params=_params("parallel", "parallel"),
        name="gdn_out",
    )(o2, o2, u, o_norm.reshape(1, GDN_HEAD_DIM).astype(F32), w, gain.reshape(1, D).astype(F32), h)


def _hy_block(Tp):
    P = 768 if Tp >= 1536 else 128
    return P, -(-Tp // P)


def _dft_mats(P):
    k = np.arange(P, dtype=np.int64)[:, None]
    b = np.arange(P, dtype=np.int64)[None, :]

    def mat(m):
        ang = np.pi * (((2 * k + 1) * m) % (4 * P)).astype(np.float64) / (2 * P)
        return np.concatenate([np.cos(ang), -np.sin(ang)], axis=0).astype(np.float32)

    fwd = mat(b)
    lo = mat(b - P)
    as_bf16 = lambda a: jnp.asarray(a).astype(BF16)
    return as_bf16(fwd), as_bf16(lo), as_bf16(np.ascontiguousarray(fwd.T))


def _hy_filter_kernel(band_ref, w1_ref, b1_ref, w2_ref, b2_ref, w3_ref, b3_ref, fr_ref,
                      w4_ref, w4b_ref, dl_ref, g_ref, n_ref, *, P, J, T):
    i = pl.program_id(0)
    hi = lax.Precision.HIGHEST
    d = i * P + lax.broadcasted_iota(jnp.int32, (P, 1), 0) - J * P
    s = jnp.abs(d).astype(F32)
    t = s / (T - 1)
    w = (2.0 * math.pi) * s / T
    lane = lax.broadcasted_iota(jnp.int32, (1, LANE), 1)
    arg = band_ref[...] * w
    z = jnp.where(lane == 0, t,
                  jnp.where(lane <= HY_BANDS, jnp.cos(arg),
                            jnp.where(lane <= 2 * HY_BANDS, -jnp.sin(arg), 0.0)))
    fr = fr_ref[...]
    hdn = jnp.sin(fr * (jnp.dot(z, w1_ref[...], precision=hi, preferred_element_type=F32) + b1_ref[...]))
    hdn = jnp.sin(fr * (jnp.dot(hdn, w2_ref[...], precision=hi, preferred_element_type=F32) + b2_ref[...]))
    hdn = jnp.sin(fr * (jnp.dot(hdn, w3_ref[...], precision=hi, preferred_element_type=F32) + b3_ref[...]))
    filt = jnp.dot(hdn, w4_ref[...], precision=hi, preferred_element_type=F32) * jnp.exp(-t * dl_ref[...])
    g = jnp.where(s < T, filt, 0.0)
    g_ref[...] = g
    part = jnp.sum(jnp.abs(g), axis=0, keepdims=True)

    @pl.when(i == 0)
    def _():
        n_ref[...] = jnp.zeros_like(n_ref)

    n_ref[...] += part

    @pl.when(i == J)
    def _():
        extra = jnp.dot(hdn[0:8], w4b_ref[...], precision=hi, preferred_element_type=F32)
        n_ref[...] += jnp.abs(extra[0:1])


def _hy_filter(f_w1, f_b1, f_w2, f_b2, f_w3, f_b3, f_w4, f_freq, E, T, P, J):
    pad2 = lambda a: jnp.zeros((LANE, LANE), F32).at[:a.shape[0], :a.shape[1]].set(a.astype(F32))
    padv = lambda a: jnp.zeros((1, LANE), F32).at[0, :a.shape[0]].set(a.astype(F32))
    bands = jnp.linspace(1e-4, HY_BANDS - 1, HY_BANDS, dtype=F32)
    band = jnp.zeros((1, LANE), F32).at[0, 1:1 + HY_BANDS].set(bands).at[0, 1 + HY_BANDS:1 + 2 * HY_BANDS].set(bands)
    w4 = jnp.zeros((LANE, 2 * E), F32).at[:HY_HIDDEN].set(f_w4.astype(F32))
    max_decay = math.log(HY_DECAY_TARGET) / HY_SHORT_DECAY_PCT
    min_decay = math.log(HY_DECAY_TARGET) / HY_LONG_DECAY_PCT
    deltas = jnp.abs(jnp.linspace(min_decay, max_decay, E, dtype=F32)).reshape(1, E)
    sq = pl.BlockSpec((LANE, LANE), lambda i: (0, 0))
    vec = pl.BlockSpec((1, LANE), lambda i: (0, 0))
    return pl.pallas_call(
        functools.partial(_hy_filter_kernel, P=P, J=J, T=T),
        grid=(2 * J,),
        in_specs=[vec, sq, vec, sq, vec, sq, vec, vec,
                  pl.BlockSpec((LANE, E), lambda i: (0, jnp.where(i >= J, 0, 1))),
                  pl.BlockSpec((LANE, E), lambda i: (0, 1)),
                  pl.BlockSpec((1, E), lambda i: (0, 0))],
        out_specs=[pl.BlockSpec((P, E), lambda i: (i, 0)),
                   pl.BlockSpec((1, E), lambda i: (0, 0))],
        out_shape=[jax.ShapeDtypeStruct((2 * J * P, E), F32),
                   jax.ShapeDtypeStruct((1, E), F32)],
        compiler_params=_params("arbitrary"),
        name="hyena_filter",
    )(band, pad2(f_w1), padv(f_b1), pad2(f_w2), padv(f_b2), pad2(f_w3), padv(f_b3), padv(f_freq),
      w4, w4, deltas)


def _hy_spec_kernel(lo_ref, hi_ref, flo_ref, fhi_ref, n_ref, gr_ref, gi_ref, *, P):
    spec = (jnp.dot(flo_ref[...], lo_ref[...].astype(BF16), preferred_element_type=F32)
            + jnp.dot(fhi_ref[...], hi_ref[...].astype(BF16), preferred_element_type=F32))
    scale = 1.0 / ((n_ref[...] + HY_FILTER_EPS) * P)
    gr_ref[0] = spec[:P] * scale
    gi_ref[0] = spec[P:] * scale


def _hy_spectrum(g, norm, fwd, lo, P, J):
    E = g.shape[1]
    cs = _tile(E, 256, LANE)
    mat = pl.BlockSpec((2 * P, P), lambda c, q: (0, 0))
    out = pl.BlockSpec((1, P, cs), lambda c, q: (q, 0, c))
    shape = jax.ShapeDtypeStruct((2 * J - 1, P, E), F32)
    return pl.pallas_call(
        functools.partial(_hy_spec_kernel, P=P),
        grid=(E // cs, 2 * J - 1),
        in_specs=[pl.BlockSpec((P, cs), lambda c, q: (q, c)),
                  pl.BlockSpec((P, cs), lambda c, q: (q + 1, c)),
                  mat, mat,
                  pl.BlockSpec((1, cs), lambda c, q: (0, c))],
        out_specs=[out, out],
        out_shape=[shape, shape],
        compiler_params=_params("parallel", "parallel"),
        name="hyena_spectrum",
    )(g, g, lo, fwd, norm)


def _hy_prep_kernel(x0m, x0l, x0r, x1m, x1l, x1r, vm, vl, vr, z_ref, cw_ref, cb_ref,
                    w_ref, gg_ref, U_ref, *, tt, Tp, ce):
    row0 = pl.program_id(1) * tt
    valid = _row_valid(row0, tt, Tp)

    def conv(k, main, left, right):
        def masked(ref, base, n):
            return jnp.where(_row_valid(base, n, Tp), ref[0].astype(F32), 0.0)

        U_ref[0:HALO] = masked(left, row0 - HALO, HALO)
        U_ref[HALO:HALO + tt] = masked(main, row0, tt)
        U_ref[HALO + tt:HALO + tt + HALO] = masked(right, row0 + tt, HALO)
        cw = cw_ref[k]
        return (cw[0:1] * U_ref[HALO - 1:HALO - 1 + tt] + cw[1:2] * U_ref[HALO:HALO + tt]
                + cw[2:3] * U_ref[HALO + 1:HALO + 1 + tt] + cb_ref[k])

    x1 = conv(1, x1m, x1l, x1r)
    v = conv(2, vm, vl, vr)
    w_ref[0] = jnp.where(valid, v * x1, 0.0)
    x0 = conv(0, x0m, x0l, x0r)
    gg_ref[0] = jnp.where(valid, x0 * _silu(z_ref[0].astype(F32)), 0.0).astype(gg_ref.dtype)


def _hy_prep(u, conv_w, conv_b, E, TH):
    B, Tp, _ = u.shape
    ce = _tile(E, 512, LANE)
    nce = E // ce
    tt = _tile(math.gcd(Tp, TH), 384, HALO)
    specs = []
    for s in range(3):
        specs += _halo_specs(ce, tt, Tp, lambda c, s=s: s * nce + c)
    nt = Tp // tt
    cw = conv_w.astype(F32).reshape(HY_SHORT, 3, E).transpose(1, 0, 2)
    cb = conv_b.astype(F32).reshape(3, 1, E)
    return pl.pallas_call(
        functools.partial(_hy_prep_kernel, tt=tt, Tp=Tp, ce=ce),
        grid=(B, TH // tt, nce),
        in_specs=[*specs,
                  pl.BlockSpec((1, tt, ce), lambda b, i, c: (b, jnp.minimum(i, nt - 1), 3 * nce + c)),
                  pl.BlockSpec((3, HY_SHORT, ce), lambda b, i, c: (0, 0, c)),
                  pl.BlockSpec((3, 1, ce), lambda b, i, c: (0, 0, c))],
        out_specs=[pl.BlockSpec((1, tt, ce), lambda b, i, c: (b, i, c)),
                   pl.BlockSpec((1, tt, ce), lambda b, i, c: (b, i, c))],
        out_shape=[jax.ShapeDtypeStruct((B, TH, E), F32),
                   jax.ShapeDtypeStruct((B, TH, E), BF16)],
        scratch_shapes=[pltpu.VMEM((tt + 2 * HALO, ce), F32)],
        compiler_params=_params("parallel", "parallel", "parallel"),
        name="hyena_prep",
    )(*([u] * 10), cw, cb)


def _hy_conv_kernel(w_ref, gg_ref, fwd_ref, inv_ref, gr_ref, gi_ref, sk_ref, o_ref, yr_ref, yi_ref,
                    *, P, J):
    s = pl.program_id(2)
    nb = w_ref.shape[0]

    @pl.when(s == 0)
    def _():
        yr_ref[...] = jnp.zeros_like(yr_ref)
        yi_ref[...] = jnp.zeros_like(yi_ref)

    @pl.when(s < J)
    def _():
        wv = jnp.concatenate([w_ref[b] for b in range(nb)], axis=1).astype(BF16)
        spec = jnp.dot(fwd_ref[...], wv, preferred_element_type=F32)
        vr, vi = spec[:P], spec[P:]
        for i in range(J):
            q = i - s + J - 1
            gr = jnp.concatenate([gr_ref[q]] * nb, axis=1)
            gi = jnp.concatenate([gi_ref[q]] * nb, axis=1)
            yr_ref[i] += vr * gr - vi * gi
            yi_ref[i] += vr * gi + vi * gr

    @pl.when(s >= J)
    def _():
        i = s - J
        spec = jnp.concatenate([yr_ref[i], yi_ref[i]], axis=0).astype(BF16)
        y = jnp.dot(inv_ref[...], spec, preferred_element_type=F32)
        for b in range(nb):
            yb = y[:, b * LANE:(b + 1) * LANE] + sk_ref[...] * w_ref[b]
            o_ref[b] = (yb * gg_ref[b].astype(F32)).astype(o_ref.dtype)


def _hy_long_conv(w, gg, fwd, inv, gr, gi, skip, P, J):
    B, TH, E = w.shape
    nb = 2 if B % 2 == 0 else 1
    blk = lambda fn: pl.BlockSpec((nb, P, LANE), fn)
    spectra = pl.BlockSpec((2 * J - 1, P, LANE), lambda c, b, s: (0, 0, c))
    return pl.pallas_call(
        functools.partial(_hy_conv_kernel, P=P, J=J),
        grid=(E // LANE, B // nb, 2 * J),
        in_specs=[blk(lambda c, b, s: (b, jnp.where(s < J, s, s - J), c)),
                  blk(lambda c, b, s: (b, jnp.maximum(s - J, 0), c)),
                  pl.BlockSpec((2 * P, P), lambda c, b, s: (0, 0)),
                  pl.BlockSpec((P, 2 * P), lambda c, b, s: (0, 0)),
                  spectra, spectra,
                  pl.BlockSpec((1, LANE), lambda c, b, s: (0, c))],
        out_specs=blk(lambda c, b, s: (b, jnp.maximum(s - J, 0), c)),
        out_shape=jax.ShapeDtypeStruct((B, TH, E), BF16),
        scratch_shapes=[pltpu.VMEM((J, P, nb * LANE), F32), pltpu.VMEM((J, P, nb * LANE), F32)],
        compiler_params=_params("parallel", "parallel", "arbitrary"),
        name="hyena_long_conv",
    )(w, gg, fwd, inv, gr, gi, skip.reshape(1, E).astype(F32))


def _hyena_layer(h, gp, go, w_in, b_in, conv_w, conv_b, f_w1, f_b1, f_w2, f_b2, f_w3, f_b3, f_w4,
                 f_freq, skip, w_out):
    B, Tp, D = h.shape
    E = w_out.shape[0]
    T = Tp - FRONT
    P, J = _hy_block(Tp)
    u = _in_proj(h, gp, w_in.astype(BF16), b_in)
    g, norm = _hy_filter(f_w1, f_b1, f_w2, f_b2, f_w3, f_b3, f_w4, f_freq, E, T, P, J)
    fwd, lo, inv = _dft_mats(P)
    gr, gi = _hy_spectrum(g, norm, fwd, lo, P, J)
    w, gg = _hy_prep(u, conv_w, conv_b, E, J * P)
    a = _hy_long_conv(w, gg, fwd, inv, gr, gi, skip, P, J)
    return _out_proj(a, w_out.astype(BF16), jnp.zeros((D,), F32), go, h)


def _attention_layer(h, gp, go, w_in, lam_vecs, subln, w_out, layer_idx):
    B, Tp, D = h.shape
    E = w_out.shape[0]
    n_qk = (w_in.shape[1] - 2 * E) // 2
    u = _in_proj(h, gp, w_in.astype(BF16), jnp.zeros((w_in.shape[1],), F32))
    qk = _rope(u, n_qk, Tp)
    a = _diff_attention_core(u, qk, lam_vecs, subln, layer_idx, E)
    return _out_proj(a, w_out.astype(BF16), jnp.zeros((D,), F32), go, h)


def _gdn_layer(h, gp, go, w_in, conv_w, a_log, dt_bias, o_norm, w_out):
    B, Tp, D = h.shape
    E = w_out.shape[0]
    n_in = w_in.shape[1]
    n_conv = conv_w.shape[1]
    n_qk = (n_conv - E) // 2
    n_pad = -(-n_in // LANE) * LANE
    w_p = jnp.zeros((D, n_pad), BF16).at[:, :n_in].set(w_in.astype(BF16))
    u = _in_proj(h, gp, w_p, jnp.zeros((n_pad,), F32))
    q, k, kt, v, gb, gbt = _gdn_prep(u, conv_w, a_log, dt_bias, n_qk, E)
    o2 = _gdn_scan(q, k, kt, v, gb, gbt)
    return _gdn_out(o2, u, n_conv // E, o_norm, w_out.astype(BF16), go, h)


def _conformer_layer(h, gp, go, w_in, b_in, dw_w, dw_b, ln_g, ln_b, w_out, b_out):
    u = _in_proj(h, gp, w_in.astype(BF16), b_in)
    a = _conformer_core(u, dw_w, dw_b, ln_g, ln_b)
    return _out_proj(a, w_out.astype(BF16), b_out, go, h)


def kernel(x, meta, norm_pre, norm_post, hy_w_in, hy_b_in, hy_conv_w, hy_conv_b, hy_f_w1, hy_f_b1, hy_f_w2, hy_f_b2, hy_f_w3, hy_f_b3, hy_f_w4, hy_f_freq, hy_skip, hy_w_out, da_w_in, da_lambda, da_subln, da_w_out, gdn_w_in, gdn_conv_w, gdn_a_log, gdn_dt_bias, gdn_o_norm, gdn_w_out, cf_w_in, cf_b_in, cf_dw_w, cf_dw_b, cf_ln_g, cf_ln_b, cf_w_out, cf_b_out):
    B, S, D = x.shape
    assert S % LANE == 0
    h = jnp.concatenate([jnp.zeros((B, FRONT, D), F32),
                         jnp.broadcast_to(meta[None].astype(F32), (B, N_META, D)),
                         x.astype(F32)], axis=1)
    for i in range(norm_pre.shape[0]):
        m, j = i % 4, i // 4
        gp, go = norm_pre[i], norm_post[i]
        if m == 0:
            h = _hyena_layer(h, gp, go, hy_w_in[j], hy_b_in[j], hy_conv_w[j], hy_conv_b[j],
                             hy_f_w1[j], hy_f_b1[j], hy_f_w2[j], hy_f_b2[j], hy_f_w3[j], hy_f_b3[j],
                             hy_f_w4[j], hy_f_freq[j], hy_skip[j], hy_w_out[j])
        elif m == 1:
            h = _attention_layer(h, gp, go, da_w_in[j], da_lambda[j], da_subln[j], da_w_out[j], i)
        elif m == 2:
            h = _gdn_layer(h, gp, go, gdn_w_in[j], gdn_conv_w[j], gdn_a_log[j], gdn_dt_bias[j],
                           gdn_o_norm[j], gdn_w_out[j])
        else:
            h = _conformer_layer(h, gp, go, cf_w_in[j], cf_b_in[j], cf_dw_w[j], cf_dw_b[j],
                                 cf_ln_g[j], cf_ln_b[j], cf_w_out[j], cf_b_out[j])
    return h[:, FRONT + N_META:].astype(x.dtype)
```

```python
import functools
import math

import numpy as np
import jax
import jax.numpy as jnp
from jax import lax
from jax.experimental import pallas as pl
from jax.experimental.pallas import tpu as pltpu

F32 = jnp.float32
BF16 = jnp.bfloat16

N_META = 16
FRONT = 112
HALO = 16
LANE = 128
RMS_EPS = 1e-6
LN_EPS = 1e-5
ROPE_THETA = 10000.0
VMEM_LIMIT = 56 * 1024 * 1024

HY_SHORT = 3
HY_BANDS = 16
HY_HIDDEN = 64
HY_SHORT_DECAY_PCT = 0.3
HY_LONG_DECAY_PCT = 1.5
HY_DECAY_TARGET = 1e-2
HY_FILTER_EPS = 1e-6
DA_HEAD_DIM = 64
DA_V_DIM = 128
DA_NORM_EPS = 1e-5
GDN_HEAD_DIM = 128
GDN_CHUNK = 128
CF_WIDTH = 31

_NT = (((1,), (1,)), ((), ()))


def _tile(n, target, mult=8):
    best = None
    for t in range(mult, min(n, target) + 1, mult):
        if n % t == 0:
            best = t
    assert best is not None, (n, target, mult)
    return best


def _params(*sem):
    return pltpu.CompilerParams(dimension_semantics=sem, vmem_limit_bytes=VMEM_LIMIT)


def _silu(x):
    return x * jax.nn.sigmoid(x)


def _row_valid(base, n, Tp):
    rows = base + lax.broadcasted_iota(jnp.int32, (n, 1), 0)
    return (rows >= FRONT) & (rows < Tp)


def _in_proj_kernel(h_ref, g_ref, w_ref, b_ref, o_ref):
    x = h_ref[0]
    ms = jnp.mean(x * x, axis=-1, keepdims=True)
    y = (x * lax.rsqrt(ms + RMS_EPS) * g_ref[...]).astype(BF16)
    acc = jnp.dot(y, w_ref[...], preferred_element_type=F32)
    o_ref[0] = (acc + b_ref[...]).astype(o_ref.dtype)


def _in_proj(h, gain, w, bias):
    B, Tp, D = h.shape
    N = w.shape[1]
    tm = _tile(Tp, 1056)
    tn = _tile(N, 2048, LANE)
    return pl.pallas_call(
        _in_proj_kernel,
        grid=(N // tn, B, Tp // tm),
        in_specs=[
            pl.BlockSpec((1, tm, D), lambda n, b, i: (b, i, 0)),
            pl.BlockSpec((1, D), lambda n, b, i: (0, 0)),
            pl.BlockSpec((D, tn), lambda n, b, i: (0, n)),
            pl.BlockSpec((1, tn), lambda n, b, i: (0, n)),
        ],
        out_specs=pl.BlockSpec((1, tm, tn), lambda n, b, i: (b, i, n)),
        out_shape=jax.ShapeDtypeStruct((B, Tp, N), BF16),
        compiler_params=_params("parallel", "parallel", "parallel"),
        name="in_proj",
    )(h, gain.reshape(1, D).astype(F32), w, bias.reshape(1, N).astype(F32))


def _post_norm_residual(y, g_ref, h_ref, o_ref):
    ms = jnp.mean(y * y, axis=-1, keepdims=True)
    o_ref[0] = h_ref[0] + y * lax.rsqrt(ms + RMS_EPS) * g_ref[...]


def _out_proj_kernel(a_ref, w_ref, b_ref, g_ref, h_ref, o_ref):
    y = jnp.dot(a_ref[0], w_ref[...], preferred_element_type=F32) + b_ref[...]
    _post_norm_residual(y, g_ref, h_ref, o_ref)


def _out_proj(a, w, bias, gain, h):
    B, Tp, D = h.shape
    E = a.shape[2]
    tm = _tile(Tp, 528)
    return pl.pallas_call(
        _out_proj_kernel,
        grid=(B, Tp // tm),
        in_specs=[
            pl.BlockSpec((1, tm, E), lambda b, i: (b, i, 0)),
            pl.BlockSpec((E, D), lambda b, i: (0, 0)),
            pl.BlockSpec((1, D), lambda b, i: (0, 0)),
            pl.BlockSpec((1, D), lambda b, i: (0, 0)),
            pl.BlockSpec((1, tm, D), lambda b, i: (b, i, 0)),
        ],
        out_specs=pl.BlockSpec((1, tm, D), lambda b, i: (b, i, 0)),
        out_shape=jax.ShapeDtypeStruct((B, Tp, D), F32),
        compiler_params=_params("parallel", "parallel"),
        name="out_proj",
    )(a, w, bias.reshape(1, D).astype(F32), gain.reshape(1, D).astype(F32), h)


def _halo_specs(width, tt, Tp, col):
    nt, nh, r = Tp // tt, Tp // HALO, tt // HALO
    main = pl.BlockSpec((1, tt, width), lambda b, i, *_: (b, jnp.minimum(i, nt - 1), col(*_)))
    left = pl.BlockSpec((1, HALO, width),
                        lambda b, i, *_: (b, jnp.clip(i * r - 1, 0, nh - 1), col(*_)))
    right = pl.BlockSpec((1, HALO, width),
                         lambda b, i, *_: (b, jnp.clip((i + 1) * r, 0, nh - 1), col(*_)))
    return main, left, right


CF_ROWS = 64


def _cf_kernel(am, al, ar, gm, gl, gr, z_ref, w_ref, b_ref, lg_ref, lb_ref, o_ref, G_ref, Y_ref,
               *, tt, Tp, E):
    row0 = pl.program_id(1) * tt

    def glu(a_ref, g_ref, base, n):
        a = a_ref[0].astype(F32)
        g = g_ref[0].astype(F32)
        return jnp.where(_row_valid(base, n, Tp), a * jax.nn.sigmoid(g), 0.0)

    G_ref[0:HALO] = glu(al, gl, row0 - HALO, HALO)
    G_ref[HALO:HALO + tt] = glu(am, gm, row0, tt)
    G_ref[HALO + tt:HALO + tt + HALO] = glu(ar, gr, row0 + tt, HALO)

    shift = HALO - CF_WIDTH // 2

    def lane_body(c, carry):
        off = pl.multiple_of(c * LANE, LANE)
        lanes = pl.ds(off, LANE)
        bias = b_ref[:, lanes]
        for rb in range(tt // CF_ROWS):
            acc = jnp.zeros((CF_ROWS, LANE), F32)
            for k in range(CF_WIDTH):
                acc = acc + w_ref[k:k + 1, lanes] * G_ref[pl.ds(rb * CF_ROWS + k + shift, CF_ROWS), lanes]
            Y_ref[pl.ds(rb * CF_ROWS, CF_ROWS), lanes] = acc + bias
        return carry

    lax.fori_loop(0, E // LANE, lane_body, 0)

    y = Y_ref[...]
    mu = jnp.mean(y, axis=-1, keepdims=True)
    yc = y - mu
    var = jnp.mean(yc * yc, axis=-1, keepdims=True)
    yn = yc * lax.rsqrt(var + LN_EPS) * lg_ref[...] + lb_ref[...]
    o_ref[0] = (_silu(yn) * _silu(z_ref[0].astype(F32))).astype(o_ref.dtype)


def _conformer_core(u, dw_w, dw_b, ln_g, ln_b):
    B, Tp, E3 = u.shape
    E = E3 // 3
    tt = _tile(Tp, 384, CF_ROWS)
    a_specs = _halo_specs(E, tt, Tp, lambda: 0)
    g_specs = _halo_specs(E, tt, Tp, lambda: 1)
    vec = pl.BlockSpec((1, E), lambda b, i: (0, 0))
    return pl.pallas_call(
        functools.partial(_cf_kernel, tt=tt, Tp=Tp, E=E),
        grid=(B, Tp // tt),
        in_specs=[*a_specs, *g_specs,
                  pl.BlockSpec((1, tt, E), lambda b, i: (b, i, 2)),
                  pl.BlockSpec((CF_WIDTH, E), lambda b, i: (0, 0)),
                  vec, vec, vec],
        out_specs=pl.BlockSpec((1, tt, E), lambda b, i: (b, i, 0)),
        out_shape=jax.ShapeDtypeStruct((B, Tp, E), BF16),
        scratch_shapes=[pltpu.VMEM((tt + 2 * HALO, E), F32), pltpu.VMEM((tt, E), F32)],
        compiler_params=_params("parallel", "parallel"),
        name="conformer_conv",
    )(u, u, u, u, u, u, u, dw_w.astype(F32), dw_b.reshape(1, E).astype(F32),
      ln_g.reshape(1, E).astype(F32), ln_b.reshape(1, E).astype(F32))


def _rope_kernel(u_ref, c_ref, s_ref, o_ref, *, n_q, scale):
    cos = c_ref[...]
    sin = s_ref[...]
    lane = lax.broadcasted_iota(jnp.int32, (1, LANE), 1)
    first = (lane % DA_HEAD_DIM) < DA_HEAD_DIM // 2
    for g in range(u_ref.shape[2] // LANE):
        x = u_ref[0, :, g * LANE:(g + 1) * LANE].astype(F32)
        partner = jnp.where(first, pltpu.roll(x, LANE - DA_HEAD_DIM // 2, axis=1),
                            pltpu.roll(x, DA_HEAD_DIM // 2, axis=1))
        r = x * cos + partner * sin
        if g < n_q:
            r = r * scale
        o_ref[0, :, g * LANE:(g + 1) * LANE] = r.astype(o_ref.dtype)


def _rope(u, n_qk, Tp):
    B = u.shape[0]
    half = DA_HEAD_DIM // 2
    inv_freq = ROPE_THETA ** (-jnp.arange(0, DA_HEAD_DIM, 2, dtype=F32) / DA_HEAD_DIM)
    pos = (jnp.arange(Tp, dtype=jnp.int32) - FRONT).astype(F32)
    ang = pos[:, None] * inv_freq[None, :]
    cos = jnp.tile(jnp.cos(ang), (1, LANE // half))
    sin = jnp.tile(jnp.concatenate([-jnp.sin(ang), jnp.sin(ang)], axis=1), (1, LANE // DA_HEAD_DIM))
    tm = _tile(Tp, 384)
    return pl.pallas_call(
        functools.partial(_rope_kernel, n_q=n_qk // LANE, scale=DA_HEAD_DIM ** -0.5),
        grid=(B, Tp // tm),
        in_specs=[pl.BlockSpec((1, tm, 2 * n_qk), lambda b, i: (b, i, 0)),
                  pl.BlockSpec((tm, LANE), lambda b, i: (i, 0)),
                  pl.BlockSpec((tm, LANE), lambda b, i: (i, 0))],
        out_specs=pl.BlockSpec((1, tm, 2 * n_qk), lambda b, i: (b, i, 0)),
        out_shape=jax.ShapeDtypeStruct((B, Tp, 2 * n_qk), BF16),
        compiler_params=_params("parallel", "parallel"),
        name="rope",
    )(u, cos, sin)


def _attn_kernel(lv_ref, sub_ref, q_ref, k_ref, v_ref, z_ref, o_ref, *, lam_init, Tp):
    q = q_ref[0]
    k = k_ref[0]
    v = v_ref[0]
    lane = lax.broadcasted_iota(jnp.int32, (1, LANE), 1)
    kvalid = lax.broadcasted_iota(jnp.int32, (1, Tp), 1) >= FRONT
    outs = []
    for m in range(2):
        sel = (lane < DA_HEAD_DIM) if m == 0 else (lane >= DA_HEAD_DIM)
        qm = jnp.where(sel, q, jnp.zeros_like(q))
        s = lax.dot_general(qm, k, _NT, preferred_element_type=F32)
        s = jnp.where(kvalid, s, -1e30)
        mx = jnp.max(s, axis=-1, keepdims=True)
        p = jnp.exp(s - mx)
        l = jnp.sum(p, axis=-1, keepdims=True)
        outs.append(jnp.dot(p.astype(BF16), v, preferred_element_type=F32) / l)
    lv = lv_ref[...]
    lam = (jnp.exp(jnp.sum(lv[0:1] * lv[1:2], axis=-1, keepdims=True))
           - jnp.exp(jnp.sum(lv[2:3] * lv[3:4], axis=-1, keepdims=True)) + lam_init)
    o = outs[0] - lam * outs[1]
    ms = jnp.mean(o * o, axis=-1, keepdims=True)
    o = o * lax.rsqrt(ms + DA_NORM_EPS) * sub_ref[...] * (1.0 - lam_init)
    o_ref[0] = (o * _silu(z_ref[0].astype(F32))).astype(o_ref.dtype)


def _diff_attention_core(u, qk, lam_vecs, subln, layer_idx, E):
    B, Tp, _ = u.shape
    H = E // DA_V_DIM
    tq = _tile(Tp, 384)
    lam_init = 0.8 - 0.6 * math.exp(-0.3 * layer_idx)
    return pl.pallas_call(
        functools.partial(_attn_kernel, lam_init=lam_init, Tp=Tp),
        grid=(B, H, Tp // tq),
        in_specs=[
            pl.BlockSpec((4, DA_HEAD_DIM), lambda b, h, i: (0, 0)),
            pl.BlockSpec((1, DA_V_DIM), lambda b, h, i: (0, 0)),
            pl.BlockSpec((1, tq, LANE), lambda b, h, i: (b, i, h)),
            pl.BlockSpec((1, Tp, LANE), lambda b, h, i: (b, 0, H + h)),
            pl.BlockSpec((1, Tp, LANE), lambda b, h, i: (b, 0, 2 * H + h)),
            pl.BlockSpec((1, tq, LANE), lambda b, h, i: (b, i, 3 * H + h)),
        ],
        out_specs=pl.BlockSpec((1, tq, LANE), lambda b, h, i: (b, i, h)),
        out_shape=jax.ShapeDtypeStruct((B, Tp, E), BF16),
        compiler_params=_params("parallel", "parallel", "parallel"),
        name="diff_attention",
    )(lam_vecs.astype(F32), subln.reshape(1, DA_V_DIM).astype(F32), qk, qk, u, u)


def _gdn_prep_kernel(um, ul, ur, ab_ref, cw_ref, al_ref, dt_ref,
                     q_ref, k_ref, kt_ref, v_ref, gb_ref, gbt_ref, U_ref, *, tt, Tp, n_qk, n_g):
    row0 = pl.program_id(1) * tt

    def masked(ref, base, n):
        return jnp.where(_row_valid(base, n, Tp), ref[0].astype(F32), 0.0)

    U_ref[0:HALO] = masked(ul, row0 - HALO, HALO)
    U_ref[HALO:HALO + tt] = masked(um, row0, tt)
    U_ref[HALO + tt:HALO + tt + HALO] = masked(ur, row0 + tt, HALO)
    valid = _row_valid(row0, tt, Tp)

    def conv_silu(c0, width):
        cols = slice(c0, c0 + width)
        y = (cw_ref[0:1, cols] * U_ref[HALO - 1:HALO - 1 + tt, cols]
             + cw_ref[1:2, cols] * U_ref[HALO:HALO + tt, cols]
             + cw_ref[2:3, cols] * U_ref[HALO + 1:HALO + 1 + tt, cols])
        return _silu(y)

    def l2n(x):
        return x * lax.rsqrt(jnp.sum(x * x, axis=-1, keepdims=True) + 1e-6)

    for hh in range(n_qk // LANE):
        c0 = hh * LANE
        qh = l2n(conv_silu(c0, LANE)) * GDN_HEAD_DIM ** -0.5
        q_ref[0, :, c0:c0 + LANE] = jnp.where(valid, qh, 0.0).astype(q_ref.dtype)
        kh = jnp.where(valid, l2n(conv_silu(n_qk + c0, LANE)), 0.0)
        k_ref[0, :, c0:c0 + LANE] = kh.astype(k_ref.dtype)
        kt_ref[0, c0:c0 + LANE, :] = kh.T.astype(kt_ref.dtype)
    n_v = v_ref.shape[2]
    for c0 in range(0, n_v, LANE):
        vh = conv_silu(2 * n_qk + c0, LANE)
        v_ref[0, :, c0:c0 + LANE] = jnp.where(valid, vh, 0.0).astype(v_ref.dtype)

    x = ab_ref[0].astype(F32)
    lane = lax.broadcasted_iota(jnp.int32, (1, LANE), 1)
    xa = x + dt_ref[...]
    softplus = jnp.maximum(xa, 0.0) + jnp.log(1.0 + jnp.exp(-jnp.abs(xa)))
    g = -jnp.exp(al_ref[...]) * softplus
    beta = jax.nn.sigmoid(x)
    gb = jnp.where(lane < n_g, g, jnp.where(lane < 2 * n_g, beta, 0.0))
    gb = jnp.where(valid, gb, 0.0)

    H = n_g // 2
    C = GDN_CHUNK
    r = lax.broadcasted_iota(jnp.int32, (C, C), 0)
    c = lax.broadcasted_iota(jnp.int32, (C, C), 1)
    lower = (c <= r).astype(F32)
    upper = (c >= r).astype(F32)
    hi = lax.Precision.HIGHEST

    def pack(gc, bt, tot):
        return jnp.where(lane < H, gc, jnp.where(lane < 2 * H, bt, jnp.where(lane < 3 * H, tot, 0.0)))

    for c3 in range(tt // C):
        rows = slice(c3 * C, (c3 + 1) * C)
        gch = gb[rows]
        cf = jnp.dot(lower, gch, precision=hi, preferred_element_type=F32)
        cb = jnp.dot(upper, gch, precision=hi, preferred_element_type=F32)
        tot = jnp.broadcast_to(jnp.sum(gch, axis=0, keepdims=True), (C, LANE))
        g0 = pack(cf, pltpu.roll(gch, LANE - H, axis=1), pltpu.roll(tot, 2 * H, axis=1))
        g1 = pack(pltpu.roll(cb, LANE - H, axis=1), pltpu.roll(gch, LANE - 2 * H, axis=1),
                  pltpu.roll(tot, H, axis=1))
        gb_ref[0, 0, rows, :] = g0
        gb_ref[1, 0, rows, :] = g1
        gbt_ref[0, 0, :, rows] = g0.T
        gbt_ref[1, 0, :, rows] = g1.T


def _gdn_prep(u, conv_w, a_log, dt_bias, n_qk, E):
    B, Tp, _ = u.shape
    n_conv = 2 * n_qk + E
    n_g = a_log.size
    assert 3 * (n_g // 2) <= LANE
    tt = _tile(Tp, 384, GDN_CHUNK)
    specs = _halo_specs(n_conv, tt, Tp, lambda: 0)
    al = jnp.zeros((1, LANE), F32).at[0, :n_g].set(a_log.reshape(-1).astype(F32))
    dt = jnp.zeros((1, LANE), F32).at[0, :n_g].set(dt_bias.reshape(-1).astype(F32))
    vec = pl.BlockSpec((1, LANE), lambda b, i: (0, 0))
    return pl.pallas_call(
        functools.partial(_gdn_prep_kernel, tt=tt, Tp=Tp, n_qk=n_qk, n_g=n_g),
        grid=(B, Tp // tt),
        in_specs=[*specs,
                  pl.BlockSpec((1, tt, LANE), lambda b, i: (b, i, (n_conv + E) // LANE)),
                  pl.BlockSpec((HY_SHORT, n_conv), lambda b, i: (0, 0)),
                  vec, vec],
        out_specs=[
            pl.BlockSpec((1, tt, n_qk), lambda b, i: (b, i, 0)),
            pl.BlockSpec((1, tt, n_qk), lambda b, i: (b, i, 0)),
            pl.BlockSpec((1, n_qk, tt), lambda b, i: (b, 0, i)),
            pl.BlockSpec((1, tt, E), lambda b, i: (b, i, 0)),
            pl.BlockSpec((2, 1, tt, LANE), lambda b, i: (0, b, i, 0)),
            pl.BlockSpec((2, 1, LANE, tt), lambda b, i: (0, b, 0, i)),
        ],
        out_shape=[
            jax.ShapeDtypeStruct((B, Tp, n_qk), BF16),
            jax.ShapeDtypeStruct((B, Tp, n_qk), BF16),
            jax.ShapeDtypeStruct((B, n_qk, Tp), BF16),
            jax.ShapeDtypeStruct((B, Tp, E), BF16),
            jax.ShapeDtypeStruct((2, B, Tp, LANE), F32),
            jax.ShapeDtypeStruct((2, B, LANE, Tp), F32),
        ],
        scratch_shapes=[pltpu.VMEM((tt + 2 * HALO, n_conv), F32)],
        compiler_params=_params("parallel", "parallel"),
        name="gdn_prep",
    )(u, u, u, u, conv_w.astype(F32), al, dt)


def _unit_lower_inverse(a):
    C = a.shape[0]
    row = lax.broadcasted_iota(jnp.int32, (C, C), 0)
    col = lax.broadcasted_iota(jnp.int32, (C, C), 1)
    rc = row ^ col
    p = jnp.where(rc < 8, a, 0.0)
    x = jnp.where(rc == 0, 1.0, 0.0) - p
    n = 2
    while n < 8:
        pb = p.astype(BF16)
        p = jnp.dot(pb, pb, preferred_element_type=F32)
        x = x + jnp.dot(x.astype(BF16), p.astype(BF16), preferred_element_type=F32)
        n *= 2
    b = 8
    while b < C:
        off = jnp.where((rc >= b) & (rc < 2 * b), a, 0.0).astype(BF16)
        xb = x.astype(BF16)
        x = x - jnp.dot(jnp.dot(xb, off, preferred_element_type=F32).astype(BF16), xb,
                        preferred_element_type=F32)
        b *= 2
    return x


def _gdn_chunk_kernel(q_ref, k_ref, kt_ref, v_ref, gb_ref, gbt_ref, o_ref, S_ref, *, n_heads):
    d = pl.program_id(0)

    @pl.when(pl.program_id(2) == 0)
    def _():
        S_ref[...] = jnp.zeros_like(S_ref)

    C = q_ref.shape[1]
    H = n_heads
    rep = n_heads // (q_ref.shape[2] // GDN_HEAD_DIM)
    row = lax.broadcasted_iota(jnp.int32, (C, C), 0)
    col = lax.broadcasted_iota(jnp.int32, (C, C), 1)
    ahead = (row - col) * (1 - 2 * d)
    incl = ahead >= 0
    strict = ahead > 0
    gb = gb_ref[0, 0]
    gbt = gbt_ref[0, 0]

    for kh in range(n_heads // rep):
        cols = slice(kh * LANE, (kh + 1) * LANE)
        q = q_ref[0, :, cols]
        k = k_ref[0, :, cols]
        kt = kt_ref[0, cols, :].astype(F32)
        kf = k.astype(F32)
        qf = q.astype(F32)
        kk = lax.dot_general(k, k, _NT, preferred_element_type=F32)
        qk = lax.dot_general(q, k, _NT, preferred_element_type=F32)
        for j in range(rep):
            hv = kh * rep + j
            gc = gb[:, hv:hv + 1]
            beta = gb[:, H + hv:H + hv + 1]
            gc_row = gbt[hv:hv + 1, :]
            gtot_row = gbt[2 * H + hv:2 * H + hv + 1, :]
            gtot = gtot_row[:, 0:1]
            decay = jnp.exp(jnp.where(incl, gc - gc_row, -1e30))
            a = jnp.where(strict, beta * kk * decay, 0.0)
            attn = qk * decay
            t_inv = _unit_lower_inverse(a)
            e_gc = jnp.exp(gc)
            vj = v_ref[0, :, hv * LANE:(hv + 1) * LANE].astype(F32)
            rhs = jnp.concatenate([vj * beta, kf * (beta * e_gc)], axis=1).astype(BF16)
            sol = jnp.dot(t_inv.astype(BF16), rhs, preferred_element_type=F32)
            u_val, w_val = sol[:, :LANE], sol[:, LANE:]
            s_old = S_ref[hv]
            lhs = jnp.concatenate([w_val, qf * e_gc], axis=0).astype(BF16)
            r = jnp.dot(lhs, s_old.astype(BF16), preferred_element_type=F32)
            v_new = (u_val - r[:C]).astype(BF16)
            o = r[C:] + jnp.dot(attn.astype(BF16), v_new, preferred_element_type=F32)
            k_end_t = (kt * jnp.exp(gtot_row - gc_row)).astype(BF16)
            S_ref[hv] = s_old * jnp.exp(gtot) + jnp.dot(k_end_t, v_new, preferred_element_type=F32)
            o_ref[0, 0, :, hv * LANE:(hv + 1) * LANE] = o.astype(o_ref.dtype)


def _gdn_scan(q, k, kt, v, gb, gbt):
    B, Tp, n_qk = q.shape
    E = v.shape[2]
    n_heads = E // GDN_HEAD_DIM
    C = GDN_CHUNK
    nC = Tp // C

    def cc(d, c):
        return c + d * (nC - 1 - 2 * c)

    return pl.pallas_call(
        functools.partial(_gdn_chunk_kernel, n_heads=n_heads),
        grid=(2, B, nC),
        in_specs=[
            pl.BlockSpec((1, C, n_qk), lambda d, b, c: (b, cc(d, c), 0)),
            pl.BlockSpec((1, C, n_qk), lambda d, b, c: (b, cc(d, c), 0)),
            pl.BlockSpec((1, n_qk, C), lambda d, b, c: (b, 0, cc(d, c))),
            pl.BlockSpec((1, C, E), lambda d, b, c: (b, cc(d, c), 0)),
            pl.BlockSpec((1, 1, C, LANE), lambda d, b, c: (d, b, cc(d, c), 0)),
            pl.BlockSpec((1, 1, LANE, C), lambda d, b, c: (d, b, 0, cc(d, c))),
        ],
        out_specs=pl.BlockSpec((1, 1, C, E), lambda d, b, c: (d, b, cc(d, c), 0)),
        out_shape=jax.ShapeDtypeStruct((2, B, Tp, E), BF16),
        scratch_shapes=[pltpu.VMEM((n_heads, GDN_HEAD_DIM, GDN_HEAD_DIM), F32)],
        compiler_params=_params("parallel", "parallel", "arbitrary"),
        name="gdn_scan",
    )(q, k, kt, v, gb, gbt)


def _gdn_out_kernel(of_ref, ob_ref, z_ref, on_ref, w_ref, g_ref, h_ref, o_ref):
    E = of_ref.shape[3]
    parts = []
    for c0 in range(0, E, LANE):
        o = of_ref[0, 0, :, c0:c0 + LANE].astype(F32) + ob_ref[0, 0, :, c0:c0 + LANE].astype(F32)
        ms = jnp.mean(o * o, axis=-1, keepdims=True)
        o = o * lax.rsqrt(ms + RMS_EPS) * on_ref[...]
        parts.append((o * _silu(z_ref[0, :, c0:c0 + LANE].astype(F32))).astype(BF16))
    a = jnp.concatenate(parts, axis=1)
    y = jnp.dot(a, w_ref[...], preferred_element_type=F32)
    _post_norm_residual(y, g_ref, h_ref, o_ref)


def _gdn_out(o2, u, z_col, o_norm, w, gain, h):
    B, Tp, D = h.shape
    E = o2.shape[3]
    tm = _tile(Tp, 528)
    return pl.pallas_call(
        _gdn_out_kernel,
        grid=(B, Tp // tm),
        in_specs=[
            pl.BlockSpec((1, 1, tm, E), lambda b, i: (0, b, i, 0)),
            pl.BlockSpec((1, 1, tm, E), lambda b, i: (1, b, i, 0)),
            pl.BlockSpec((1, tm, E), lambda b, i: (b, i, z_col)),
            pl.BlockSpec((1, GDN_HEAD_DIM), lambda b, i: (0, 0)),
            pl.BlockSpec((E, D), lambda b, i: (0, 0)),
            pl.BlockSpec((1, D), lambda b, i: (0, 0)),
            pl.BlockSpec((1, tm, D), lambda b, i: (b, i, 0)),
        ],
        out_specs=pl.BlockSpec((1, tm, D), lambda b, i: (b, i, 0)),
        out_shape=jax.ShapeDtypeStruct((B, Tp, D), F32),
        compiler_params=_params("parallel", "parallel"),
        name="gdn_out",
    )(o2, o2, u, o_norm.reshape(1, GDN_HEAD_DIM).astype(F32), w, gain.reshape(1, D).astype(F32), h)


def _hy_block(Tp):
    P = 768 if Tp >= 1536 else 128
    return P, -(-Tp // P)


def _dft_mats(P):
    k = np.arange(P, dtype=np.int64)[:, None]
    b = np.arange(P, dtype=np.int64)[None, :]

    def mat(m):
        ang = np.pi * (((2 * k + 1) * m) % (4 * P)).astype(np.float64) / (2 * P)
        return np.concatenate([np.cos(ang), -np.sin(ang)], axis=0).astype(np.float32)

    fwd = mat(b)
    lo = mat(b - P)
    as_bf16 = lambda a: jnp.asarray(a).astype(BF16)
    return as_bf16(fwd), as_bf16(lo), as_bf16(np.ascontiguousarray(fwd.T))


def _hy_filter_kernel(band_ref, w1_ref, b1_ref, w2_ref, b2_ref, w3_ref, b3_ref, fr_ref,
                      w4_ref, w4b_ref, dl_ref, g_ref, n_ref, *, P, J, T):
    i = pl.program_id(0)
    hi = lax.Precision.HIGHEST
    d = i * P + lax.broadcasted_iota(jnp.int32, (P, 1), 0) - J * P
    s = jnp.abs(d).astype(F32)
    t = s / (T - 1)
    w = (2.0 * math.pi) * s / T
    lane = lax.broadcasted_iota(jnp.int32, (1, LANE), 1)
    arg = band_ref[...] * w
    z = jnp.where(lane == 0, t,
                  jnp.where(lane <= HY_BANDS, jnp.cos(arg),
                            jnp.where(lane <= 2 * HY_BANDS, -jnp.sin(arg), 0.0)))
    fr = fr_ref[...]
    hdn = jnp.sin(fr * (jnp.dot(z, w1_ref[...], precision=hi, preferred_element_type=F32) + b1_ref[...]))
    hdn = jnp.sin(fr * (jnp.dot(hdn, w2_ref[...], precision=hi, preferred_element_type=F32) + b2_ref[...]))
    hdn = jnp.sin(fr * (jnp.dot(hdn, w3_ref[...], precision=hi, preferred_element_type=F32) + b3_ref[...]))
    filt = jnp.dot(hdn, w4_ref[...], precision=hi, preferred_element_type=F32) * jnp.exp(-t * dl_ref[...])
    g = jnp.where(s < T, filt, 0.0)
    g_ref[...] = g
    part = jnp.sum(jnp.abs(g), axis=0, keepdims=True)

    @pl.when(i == 0)
    def _():
        n_ref[...] = jnp.zeros_like(n_ref)

    n_ref[...] += part

    @pl.when(i == J)
    def _():
        extra = jnp.dot(hdn[0:8], w4b_ref[...], precision=hi, preferred_element_type=F32)
        n_ref[...] += jnp.abs(extra[0:1])


def _hy_filter(f_w1, f_b1, f_w2, f_b2, f_w3, f_b3, f_w4, f_freq, E, T, P, J):
    pad2 = lambda a: jnp.zeros((LANE, LANE), F32).at[:a.shape[0], :a.shape[1]].set(a.astype(F32))
    padv = lambda a: jnp.zeros((1, LANE), F32).at[0, :a.shape[0]].set(a.astype(F32))
    bands = jnp.linspace(1e-4, HY_BANDS - 1, HY_BANDS, dtype=F32)
    band = jnp.zeros((1, LANE), F32).at[0, 1:1 + HY_BANDS].set(bands).at[0, 1 + HY_BANDS:1 + 2 * HY_BANDS].set(bands)
    w4 = jnp.zeros((LANE, 2 * E), F32).at[:HY_HIDDEN].set(f_w4.astype(F32))
    max_decay = math.log(HY_DECAY_TARGET) / HY_SHORT_DECAY_PCT
    min_decay = math.log(HY_DECAY_TARGET) / HY_LONG_DECAY_PCT
    deltas = jnp.abs(jnp.linspace(min_decay, max_decay, E, dtype=F32)).reshape(1, E)
    sq = pl.BlockSpec((LANE, LANE), lambda i: (0, 0))
    vec = pl.BlockSpec((1, LANE), lambda i: (0, 0))
    return pl.pallas_call(
        functools.partial(_hy_filter_kernel, P=P, J=J, T=T),
        grid=(2 * J,),
        in_specs=[vec, sq, vec, sq, vec, sq, vec, vec,
                  pl.BlockSpec((LANE, E), lambda i: (0, jnp.where(i >= J, 0, 1))),
                  pl.BlockSpec((LANE, E), lambda i: (0, 1)),
                  pl.BlockSpec((1, E), lambda i: (0, 0))],
        out_specs=[pl.BlockSpec((P, E), lambda i: (i, 0)),
                   pl.BlockSpec((1, E), lambda i: (0, 0))],
        out_shape=[jax.ShapeDtypeStruct((2 * J * P, E), F32),
                   jax.ShapeDtypeStruct((1, E), F32)],
        compiler_params=_params("arbitrary"),
        name="hyena_filter",
    )(band, pad2(f_w1), padv(f_b1), pad2(f_w2), padv(f_b2), pad2(f_w3), padv(f_b3), padv(f_freq),
      w4, w4, deltas)


def _hy_spec_kernel(lo_ref, hi_ref, flo_ref, fhi_ref, n_ref, gr_ref, gi_ref, *, P):
    spec = (jnp.dot(flo_ref[...], lo_ref[...].astype(BF16), preferred_element_type=F32)
            + jnp.dot(fhi_ref[...], hi_ref[...].astype(BF16), preferred_element_type=F32))
    scale = 1.0 / ((n_ref[...] + HY_FILTER_EPS) * P)
    gr_ref[0] = spec[:P] * scale
    gi_ref[0] = spec[P:] * scale


def _hy_spectrum(g, norm, fwd, lo, P, J):
    E = g.shape[1]
    cs = _tile(E, 256, LANE)
    mat = pl.BlockSpec((2 * P, P), lambda c, q: (0, 0))
    out = pl.BlockSpec((1, P, cs), lambda c, q: (q, 0, c))
    shape = jax.ShapeDtypeStruct((2 * J - 1, P, E), F32)
    return pl.pallas_call(
        functools.partial(_hy_spec_kernel, P=P),
        grid=(E // cs, 2 * J - 1),
        in_specs=[pl.BlockSpec((P, cs), lambda c, q: (q, c)),
                  pl.BlockSpec((P, cs), lambda c, q: (q + 1, c)),
                  mat, mat,
                  pl.BlockSpec((1, cs), lambda c, q: (0, c))],
        out_specs=[out, out],
        out_shape=[shape, shape],
        compiler_params=_params("parallel", "parallel"),
        name="hyena_spectrum",
    )(g, g, lo, fwd, norm)


def _hy_prep_kernel(x0m, x0l, x0r, x1m, x1l, x1r, vm, vl, vr, z_ref, cw_ref, cb_ref,
                    w_ref, gg_ref, U_ref, *, tt, Tp, ce):
    row0 = pl.program_id(1) * tt
    valid = _row_valid(row0, tt, Tp)

    def conv(k, main, left, right):
        def masked(ref, base, n):
            return jnp.where(_row_valid(base, n, Tp), ref[0].astype(F32), 0.0)

        U_ref[0:HALO] = masked(left, row0 - HALO, HALO)
        U_ref[HALO:HALO + tt] = masked(main, row0, tt)
        U_ref[HALO + tt:HALO + tt + HALO] = masked(right, row0 + tt, HALO)
        cw = cw_ref[k]
        return (cw[0:1] * U_ref[HALO - 1:HALO - 1 + tt] + cw[1:2] * U_ref[HALO:HALO + tt]
                + cw[2:3] * U_ref[HALO + 1:HALO + 1 + tt] + cb_ref[k])

    x1 = conv(1, x1m, x1l, x1r)
    v = conv(2, vm, vl, vr)
    w_ref[0] = jnp.where(valid, v * x1, 0.0)
    x0 = conv(0, x0m, x0l, x0r)
    gg_ref[0] = jnp.where(valid, x0 * _silu(z_ref[0].astype(F32)), 0.0).astype(gg_ref.dtype)


def _hy_prep(u, conv_w, conv_b, E, TH):
    B, Tp, _ = u.shape
    ce = _tile(E, 512, LANE)
    nce = E // ce
    tt = _tile(math.gcd(Tp, TH), 384, HALO)
    specs = []
    for s in range(3):
        specs += _halo_specs(ce, tt, Tp, lambda c, s=s: s * nce + c)
    nt = Tp // tt
    cw = conv_w.astype(F32).reshape(HY_SHORT, 3, E).transpose(1, 0, 2)
    cb = conv_b.astype(F32).reshape(3, 1, E)
    return pl.pallas_call(
        functools.partial(_hy_prep_kernel, tt=tt, Tp=Tp, ce=ce),
        grid=(B, TH // tt, nce),
        in_specs=[*specs,
                  pl.BlockSpec((1, tt, ce), lambda b, i, c: (b, jnp.minimum(i, nt - 1), 3 * nce + c)),
                  pl.BlockSpec((3, HY_SHORT, ce), lambda b, i, c: (0, 0, c)),
                  pl.BlockSpec((3, 1, ce), lambda b, i, c: (0, 0, c))],
        out_specs=[pl.BlockSpec((1, tt, ce), lambda b, i, c: (b, i, c)),
                   pl.BlockSpec((1, tt, ce), lambda b, i, c: (b, i, c))],
        out_shape=[jax.ShapeDtypeStruct((B, TH, E), F32),
                   jax.ShapeDtypeStruct((B, TH, E), BF16)],
        scratch_shapes=[pltpu.VMEM((tt + 2 * HALO, ce), F32)],
        compiler_params=_params("parallel", "parallel", "parallel"),
        name="hyena_prep",
    )(*([u] * 10), cw, cb)


def _hy_conv_kernel(w_ref, gg_ref, fwd_ref, inv_ref, gr_ref, gi_ref, sk_ref, o_ref, yr_ref, yi_ref,
                    *, P, J):
    s = pl.program_id(2)
    nb = w_ref.shape[0]

    @pl.when(s == 0)
    def _():
        yr_ref[...] = jnp.zeros_like(yr_ref)
        yi_ref[...] = jnp.zeros_like(yi_ref)

    @pl.when(s < J)
    def _():
        wv = jnp.concatenate([w_ref[b] for b in range(nb)], axis=1).astype(BF16)
        spec = jnp.dot(fwd_ref[...], wv, preferred_element_type=F32)
        vr, vi = spec[:P], spec[P:]
        for i in range(J):
            q = i - s + J - 1
            gr = jnp.concatenate([gr_ref[q]] * nb, axis=1)
            gi = jnp.concatenate([gi_ref[q]] * nb, axis=1)
            yr_ref[i] += vr * gr - vi * gi
            yi_ref[i] += vr * gi + vi * gr

    @pl.when(s >= J)
    def _():
        i = s - J
        spec = jnp.concatenate([yr_ref[i], yi_ref[i]], axis=0).astype(BF16)
        y = jnp.dot(inv_ref[...], spec, preferred_element_type=F32)
        for b in range(nb):
            yb = y[:, b * LANE:(b + 1) * LANE] + sk_ref[...] * w_ref[b]
            o_ref[b] = (yb * gg_ref[b].astype(F32)).astype(o_ref.dtype)


def _hy_long_conv(w, gg, fwd, inv, gr, gi, skip, P, J):
    B, TH, E = w.shape
    nb = 2 if B % 2 == 0 else 1
    blk = lambda fn: pl.BlockSpec((nb, P, LANE), fn)
    spectra = pl.BlockSpec((2 * J - 1, P, LANE), lambda c, b, s: (0, 0, c))
    return pl.pallas_call(
        functools.partial(_hy_conv_kernel, P=P, J=J),
        grid=(E // LANE, B // nb, 2 * J),
        in_specs=[blk(lambda c, b, s: (b, jnp.where(s < J, s, s - J), c)),
                  blk(lambda c, b, s: (b, jnp.maximum(s - J, 0), c)),
                  pl.BlockSpec((2 * P, P), lambda c, b, s: (0, 0)),
                  pl.BlockSpec((P, 2 * P), lambda c, b, s: (0, 0)),
                  spectra, spectra,
                  pl.BlockSpec((1, LANE), lambda c, b, s: (0, c))],
        out_specs=blk(lambda c, b, s: (b, jnp.maximum(s - J, 0), c)),
        out_shape=jax.ShapeDtypeStruct((B, TH, E), BF16),
        scratch_shapes=[pltpu.VMEM((J, P, nb * LANE), F32), pltpu.VMEM((J, P, nb * LANE), F32)],
        compiler_params=_params("parallel", "parallel", "arbitrary"),
        name="hyena_long_conv",
    )(w, gg, fwd, inv, gr, gi, skip.reshape(1, E).astype(F32))


def _hyena_layer(h, gp, go, w_in, b_in, conv_w, conv_b, f_w1, f_b1, f_w2, f_b2, f_w3, f_b3, f_w4,
                 f_freq, skip, w_out):
    B, Tp, D = h.shape
    E = w_out.shape[0]
    T = Tp - FRONT
    P, J = _hy_block(Tp)
    u = _in_proj(h, gp, w_in.astype(BF16), b_in)
    g, norm = _hy_filter(f_w1, f_b1, f_w2, f_b2, f_w3, f_b3, f_w4, f_freq, E, T, P, J)
    fwd, lo, inv = _dft_mats(P)
    gr, gi = _hy_spectrum(g, norm, fwd, lo, P, J)
    w, gg = _hy_prep(u, conv_w, conv_b, E, J * P)
    a = _hy_long_conv(w, gg, fwd, inv, gr, gi, skip, P, J)
    return _out_proj(a, w_out.astype(BF16), jnp.zeros((D,), F32), go, h)


def _attention_layer(h, gp, go, w_in, lam_vecs, subln, w_out, layer_idx):
    B, Tp, D = h.shape
    E = w_out.shape[0]
    n_qk = (w_in.shape[1] - 2 * E) // 2
    u = _in_proj(h, gp, w_in.astype(BF16), jnp.zeros((w_in.shape[1],), F32))
    qk = _rope(u, n_qk, Tp)
    a = _diff_attention_core(u, qk, lam_vecs, subln, layer_idx, E)
    return _out_proj(a, w_out.astype(BF16), jnp.zeros((D,), F32), go, h)


def _gdn_layer(h, gp, go, w_in, conv_w, a_log, dt_bias, o_norm, w_out):
    B, Tp, D = h.shape
    E = w_out.shape[0]
    n_in = w_in.shape[1]
    n_conv = conv_w.shape[1]
    n_qk = (n_conv - E) // 2
    n_pad = -(-n_in // LANE) * LANE
    w_p = jnp.zeros((D, n_pad), BF16).at[:, :n_in].set(w_in.astype(BF16))
    u = _in_proj(h, gp, w_p, jnp.zeros((n_pad,), F32))
    q, k, kt, v, gb, gbt = _gdn_prep(u, conv_w, a_log, dt_bias, n_qk, E)
    o2 = _gdn_scan(q, k, kt, v, gb, gbt)
    return _gdn_out(o2, u, n_conv // E, o_norm, w_out.astype(BF16), go, h)


def _conformer_layer(h, gp, go, w_in, b_in, dw_w, dw_b, ln_g, ln_b, w_out, b_out):
    u = _in_proj(h, gp, w_in.astype(BF16), b_in)
    a = _conformer_core(u, dw_w, dw_b, ln_g, ln_b)
    return _out_proj(a, w_out.astype(BF16), b_out, go, h)


def kernel(x, meta, norm_pre, norm_post, hy_w_in, hy_b_in, hy_conv_w, hy_conv_b, hy_f_w1, hy_f_b1, hy_f_w2, hy_f_b2, hy_f_w3, hy_f_b3, hy_f_w4, hy_f_freq, hy_skip, hy_w_out, da_w_in, da_lambda, da_subln, da_w_out, gdn_w_in, gdn_conv_w, gdn_a_log, gdn_dt_bias, gdn_o_norm, gdn_w_out, cf_w_in, cf_b_in, cf_dw_w, cf_dw_b, cf_ln_g, cf_ln_b, cf_w_out, cf_b_out):
    B, S, D = x.shape
    assert S % LANE == 0
    h = jnp.concatenate([jnp.zeros((B, FRONT, D), F32),
                         jnp.broadcast_to(meta[None].astype(F32), (B, N_META, D)),
                         x.astype(F32)], axis=1)
    for i in range(norm_pre.shape[0]):
        m, j = i % 4, i // 4
        gp, go = norm_pre[i], norm_post[i]
        if m == 0:
            h = _hyena_layer(h, gp, go, hy_w_in[j], hy_b_in[j], hy_conv_w[j], hy_conv_b[j],
                             hy_f_w1[j], hy_f_b1[j], hy_f_w2[j], hy_f_b2[j], hy_f_w3[j], hy_f_b3[j],
                             hy_f_w4[j], hy_f_freq[j], hy_skip[j], hy_w_out[j])
        elif m == 1:
            h = _attention_layer(h, gp, go, da_w_in[j], da_lambda[j], da_subln[j], da_w_out[j], i)
        elif m == 2:
            h = _gdn_layer(h, gp, go, gdn_w_in[j], gdn_conv_w[j], gdn_a_log[j], gdn_dt_bias[j],
                           gdn_o_norm[j], gdn_w_out[j])
        else:
            h = _conformer_layer(h, gp, go, cf_w_in[j], cf_b_in[j], cf_dw_w[j], cf_dw_b[j],
                                 cf_ln_g[j], cf_ln_b[j], cf_w_out[j], cf_b_out[j])
    return h[:, FRONT + N_META:].astype(x.dtype)
```

```python
import functools
import math

import numpy as np
import jax
import jax.numpy as jnp
from jax import lax
from jax.experimental import pallas as pl
from jax.experimental.pallas import tpu as pltpu

F32 = jnp.float32
BF16 = jnp.bfloat16

N_META = 16
FRONT = 112
HALO = 16
LANE = 128
RMS_EPS = 1e-6
LN_EPS = 1e-5
ROPE_THETA = 10000.0
VMEM_LIMIT = 56 * 1024 * 1024

HY_SHORT = 3
HY_BANDS = 16
HY_HIDDEN = 64
HY_SHORT_DECAY_PCT = 0.3
HY_LONG_DECAY_PCT = 1.5
HY_DECAY_TARGET = 1e-2
HY_FILTER_EPS = 1e-6
DA_HEAD_DIM = 64
DA_V_DIM = 128
DA_NORM_EPS = 1e-5
GDN_HEAD_DIM = 128
GDN_CHUNK = 128
CF_WIDTH = 31

_NT = (((1,), (1,)), ((), ()))


def _tile(n, target, mult=8):
    best = None
    for t in range(mult, min(n, target) + 1, mult):
        if n % t == 0:
            best = t
    assert best is not None, (n, target, mult)
    return best


def _params(*sem):
    return pltpu.CompilerParams(dimension_semantics=sem, vmem_limit_bytes=VMEM_LIMIT)


def _silu(x):
    return x * jax.nn.sigmoid(x)


def _row_valid(base, n, Tp):
    rows = base + lax.broadcasted_iota(jnp.int32, (n, 1), 0)
    return (rows >= FRONT) & (rows < Tp)


def _in_proj_kernel(h_ref, g_ref, w_ref, b_ref, o_ref):
    x = h_ref[0]
    ms = jnp.mean(x * x, axis=-1, keepdims=True)
    y = (x * lax.rsqrt(ms + RMS_EPS) * g_ref[...]).astype(BF16)
    acc = jnp.dot(y, w_ref[...], preferred_element_type=F32)
    o_ref[0] = (acc + b_ref[...]).astype(o_ref.dtype)


def _in_proj(h, gain, w, bias):
    B, Tp, D = h.shape
    N = w.shape[1]
    tm = _tile(Tp, 1056)
    tn = _tile(N, 2048, LANE)
    return pl.pallas_call(
        _in_proj_kernel,
        grid=(N // tn, B, Tp // tm),
        in_specs=[
            pl.BlockSpec((1, tm, D), lambda n, b, i: (b, i, 0)),
            pl.BlockSpec((1, D), lambda n, b, i: (0, 0)),
            pl.BlockSpec((D, tn), lambda n, b, i: (0, n)),
            pl.BlockSpec((1, tn), lambda n, b, i: (0, n)),
        ],
        out_specs=pl.BlockSpec((1, tm, tn), lambda n, b, i: (b, i, n)),
        out_shape=jax.ShapeDtypeStruct((B, Tp, N), BF16),
        compiler_params=_params("parallel", "parallel", "parallel"),
        name="in_proj",
    )(h, gain.reshape(1, D).astype(F32), w, bias.reshape(1, N).astype(F32))


def _post_norm_residual(y, g_ref, h_ref, o_ref):
    ms = jnp.mean(y * y, axis=-1, keepdims=True)
    o_ref[0] = h_ref[0] + y * lax.rsqrt(ms + RMS_EPS) * g_ref[...]


def _out_proj_kernel(a_ref, w_ref, b_ref, g_ref, h_ref, o_ref):
    y = jnp.dot(a_ref[0], w_ref[...], preferred_element_type=F32) + b_ref[...]
    _post_norm_residual(y, g_ref, h_ref, o_ref)


def _out_proj(a, w, bias, gain, h):
    B, Tp, D = h.shape
    E = a.shape[2]
    tm = _tile(Tp, 528)
    return pl.pallas_call(
        _out_proj_kernel,
        grid=(B, Tp // tm),
        in_specs=[
            pl.BlockSpec((1, tm, E), lambda b, i: (b, i, 0)),
            pl.BlockSpec((E, D), lambda b, i: (0, 0)),
            pl.BlockSpec((1, D), lambda b, i: (0, 0)),
            pl.BlockSpec((1, D), lambda b, i: (0, 0)),
            pl.BlockSpec((1, tm, D), lambda b, i: (b, i, 0)),
        ],
        out_specs=pl.BlockSpec((1, tm, D), lambda b, i: (b, i, 0)),
        out_shape=jax.ShapeDtypeStruct((B, Tp, D), F32),
        compiler_params=_params("parallel", "parallel"),
        name="out_proj",
    )(a, w, bias.reshape(1, D).astype(F32), gain.reshape(1, D).astype(F32), h)


def _halo_specs(width, tt, Tp, col):
    nt, nh, r = Tp // tt, Tp // HALO, tt // HALO
    main = pl.BlockSpec((1, tt, width), lambda b, i, *_: (b, jnp.minimum(i, nt - 1), col(*_)))
    left = pl.BlockSpec((1, HALO, width),
                        lambda b, i, *_: (b, jnp.clip(i * r - 1, 0, nh - 1), col(*_)))
    right = pl.BlockSpec((1, HALO, width),
                         lambda b, i, *_: (b, jnp.clip((i + 1) * r, 0, nh - 1), col(*_)))
    return main, left, right


CF_ROWS = 64


def _cf_kernel(am, al, ar, gm, gl, gr, z_ref, w_ref, b_ref, lg_ref, lb_ref, o_ref, G_ref, Y_ref,
               *, tt, Tp, E):
    row0 = pl.program_id(1) * tt

    def glu(a_ref, g_ref, base, n):
        a = a_ref[0].astype(F32)
        g = g_ref[0].astype(F32)
        return jnp.where(_row_valid(base, n, Tp), a * jax.nn.sigmoid(g), 0.0)

    G_ref[0:HALO] = glu(al, gl, row0 - HALO, HALO)
    G_ref[HALO:HALO + tt] = glu(am, gm, row0, tt)
    G_ref[HALO + tt:HALO + tt + HALO] = glu(ar, gr, row0 + tt, HALO)

    shift = HALO - CF_WIDTH // 2

    def lane_body(c, carry):
        off = pl.multiple_of(c * LANE, LANE)
        lanes = pl.ds(off, LANE)
        bias = b_ref[:, lanes]
        for rb in range(tt // CF_ROWS):
            acc = jnp.zeros((CF_ROWS, LANE), F32)
            for k in range(CF_WIDTH):
                acc = acc + w_ref[k:k + 1, lanes] * G_ref[pl.ds(rb * CF_ROWS + k + shift, CF_ROWS), lanes]
            Y_ref[pl.ds(rb * CF_ROWS, CF_ROWS), lanes] = acc + bias
        return carry

    lax.fori_loop(0, E // LANE, lane_body, 0)

    y = Y_ref[...]
    mu = jnp.mean(y, axis=-1, keepdims=True)
    yc = y - mu
    var = jnp.mean(yc * yc, axis=-1, keepdims=True)
    yn = yc * lax.rsqrt(var + LN_EPS) * lg_ref[...] + lb_ref[...]
    o_ref[0] = (_silu(yn) * _silu(z_ref[0].astype(F32))).astype(o_ref.dtype)


def _conformer_core(u, dw_w, dw_b, ln_g, ln_b):
    B, Tp, E3 = u.shape
    E = E3 // 3
    tt = _tile(Tp, 384, CF_ROWS)
    a_specs = _halo_specs(E, tt, Tp, lambda: 0)
    g_specs = _halo_specs(E, tt, Tp, lambda: 1)
    vec = pl.BlockSpec((1, E), lambda b, i: (0, 0))
    return pl.pallas_call(
        functools.partial(_cf_kernel, tt=tt, Tp=Tp, E=E),
        grid=(B, Tp // tt),
        in_specs=[*a_specs, *g_specs,
                  pl.BlockSpec((1, tt, E), lambda b, i: (b, i, 2)),
                  pl.BlockSpec((CF_WIDTH, E), lambda b, i: (0, 0)),
                  vec, vec, vec],
        out_specs=pl.BlockSpec((1, tt, E), lambda b, i: (b, i, 0)),
        out_shape=jax.ShapeDtypeStruct((B, Tp, E), BF16),
        scratch_shapes=[pltpu.VMEM((tt + 2 * HALO, E), F32), pltpu.VMEM((tt, E), F32)],
        compiler_params=_params("parallel", "parallel"),
        name="conformer_conv",
    )(u, u, u, u, u, u, u, dw_w.astype(F32), dw_b.reshape(1, E).astype(F32),
      ln_g.reshape(1, E).astype(F32), ln_b.reshape(1, E).astype(F32))


def _rope_kernel(u_ref, c_ref, s_ref, o_ref, *, n_q, scale):
    cos = c_ref[...]
    sin = s_ref[...]
    lane = lax.broadcasted_iota(jnp.int32, (1, LANE), 1)
    first = (lane % DA_HEAD_DIM) < DA_HEAD_DIM // 2
    for g in range(u_ref.shape[2] // LANE):
        x = u_ref[0, :, g * LANE:(g + 1) * LANE].astype(F32)
        partner = jnp.where(first, pltpu.roll(x, LANE - DA_HEAD_DIM // 2, axis=1),
                            pltpu.roll(x, DA_HEAD_DIM // 2, axis=1))
        r = x * cos + partner * sin
        if g < n_q:
            r = r * scale
        o_ref[0, :, g * LANE:(g + 1) * LANE] = r.astype(o_ref.dtype)


def _rope(u, n_qk, Tp):
    B = u.shape[0]
    half = DA_HEAD_DIM // 2
    inv_freq = ROPE_THETA ** (-jnp.arange(0, DA_HEAD_DIM, 2, dtype=F32) / DA_HEAD_DIM)
    pos = (jnp.arange(Tp, dtype=jnp.int32) - FRONT).astype(F32)
    ang = pos[:, None] * inv_freq[None, :]
    cos = jnp.tile(jnp.cos(ang), (1, LANE // half))
    sin = jnp.tile(jnp.concatenate([-jnp.sin(ang), jnp.sin(ang)], axis=1), (1, LANE // DA_HEAD_DIM))
    tm = _tile(Tp, 384)
    return pl.pallas_call(
        functools.partial(_rope_kernel, n_q=n_qk // LANE, scale=DA_HEAD_DIM ** -0.5 * math.log2(math.e)),
        grid=(B, Tp // tm),
        in_specs=[pl.BlockSpec((1, tm, 2 * n_qk), lambda b, i: (b, i, 0)),
                  pl.BlockSpec((tm, LANE), lambda b, i: (i, 0)),
                  pl.BlockSpec((tm, LANE), lambda b, i: (i, 0))],
        out_specs=pl.BlockSpec((1, tm, 2 * n_qk), lambda b, i: (b, i, 0)),
        out_shape=jax.ShapeDtypeStruct((B, Tp, 2 * n_qk), BF16),
        compiler_params=_params("parallel", "parallel"),
        name="rope",
    )(u, cos, sin)


def _attn_kernel(lv_ref, sub_ref, q_ref, k_ref, v_ref, z_ref, o_ref, vx_ref, s_ref,
                 *, lam_init, Tp, kc):
    first = FRONT + N_META

    @pl.when(pl.program_id(2) == 0)
    def _():
        vx_ref[:, :LANE] = v_ref[0]
        vx_ref[:, LANE:] = jnp.ones((Tp, LANE), BF16)

    q = q_ref[0]
    lane = lax.broadcasted_iota(jnp.int32, (1, LANE), 1)
    zero = jnp.zeros_like(q)
    qm = [jnp.where(lane < DA_HEAD_DIM, q, zero), jnp.where(lane >= DA_HEAD_DIM, q, zero)]
    chunks = [(0, first)] + [(r0, kc) for r0 in range(first, Tp, kc)]
    kvalid = lax.broadcasted_iota(jnp.int32, (1, first), 1) >= FRONT

    def scores(m, c):
        r0, n = chunks[c]
        s = lax.dot_general(qm[m], k_ref[0, r0:r0 + n, :], _NT, preferred_element_type=F32)
        if c == 0:
            s = jnp.where(kvalid, s, -1e30)
        s_ref[:, r0:r0 + n] = s
        return s

    def weighted(mx, c):
        r0, n = chunks[c]
        p = jnp.exp2(s_ref[:, r0:r0 + n] - mx).astype(BF16)
        return jnp.dot(p, vx_ref[r0:r0 + n, :], preferred_element_type=F32)

    def row_max(first_chunk, part):
        return jnp.maximum(jnp.max(part, axis=-1, keepdims=True),
                           jnp.max(first_chunk, axis=-1, keepdims=True))

    nc = len(chunks)
    n_acc = 2

    def accumulate(accs, t, c):
        accs[c % n_acc] = t if accs[c % n_acc] is None else accs[c % n_acc] + t

    def running_max(part, s, c):
        return s if c == 1 else (part if c == 0 else jnp.maximum(part, s))

    first0 = scores(0, 0)
    part = None
    for c in range(1, nc):
        part = running_max(part, scores(0, c), c)
    m0 = row_max(first0, part)
    acc0 = [None] * n_acc
    for c in range(nc):
        accumulate(acc0, weighted(m0, c), c)
        s = scores(1, c)
        if c == 0:
            first1 = s
        part = running_max(part, s, c)
    m1 = row_max(first1, part)
    acc1 = [None] * n_acc
    for c in range(nc):
        accumulate(acc1, weighted(m1, c), c)
    outs = []
    for accs in (acc0, acc1):
        acc = functools.reduce(lambda x, y: x + y, accs)
        outs.append(acc[:, :LANE] / acc[:, LANE:LANE + 1])
    lv = lv_ref[...]
    lam = (jnp.exp(jnp.sum(lv[0:1] * lv[1:2], axis=-1, keepdims=True))
           - jnp.exp(jnp.sum(lv[2:3] * lv[3:4], axis=-1, keepdims=True)) + lam_init)
    o = outs[0] - lam * outs[1]
    ms = jnp.mean(o * o, axis=-1, keepdims=True)
    o = o * lax.rsqrt(ms + DA_NORM_EPS) * sub_ref[...] * (1.0 - lam_init)
    o_ref[0] = (o * _silu(z_ref[0].astype(F32))).astype(o_ref.dtype)


def _diff_attention_core(u, qk, lam_vecs, subln, layer_idx, E):
    B, Tp, _ = u.shape
    H = E // DA_V_DIM
    tq = _tile(Tp, 384)
    kc = 2 * LANE
    assert (Tp - FRONT - N_META) % kc == 0
    lam_init = 0.8 - 0.6 * math.exp(-0.3 * layer_idx)
    return pl.pallas_call(
        functools.partial(_attn_kernel, lam_init=lam_init, Tp=Tp, kc=kc),
        grid=(B, H, Tp // tq),
        in_specs=[
            pl.BlockSpec((4, DA_HEAD_DIM), lambda b, h, i: (0, 0)),
            pl.BlockSpec((1, DA_V_DIM), lambda b, h, i: (0, 0)),
            pl.BlockSpec((1, tq, LANE), lambda b, h, i: (b, i, h)),
            pl.BlockSpec((1, Tp, LANE), lambda b, h, i: (b, 0, H + h)),
            pl.BlockSpec((1, Tp, LANE), lambda b, h, i: (b, 0, 2 * H + h)),
            pl.BlockSpec((1, tq, LANE), lambda b, h, i: (b, i, 3 * H + h)),
        ],
        out_specs=pl.BlockSpec((1, tq, LANE), lambda b, h, i: (b, i, h)),
        out_shape=jax.ShapeDtypeStruct((B, Tp, E), BF16),
        scratch_shapes=[pltpu.VMEM((Tp, 2 * LANE), BF16), pltpu.VMEM((tq, Tp), F32)],
        compiler_params=_params("parallel", "parallel", "arbitrary"),
        name="diff_attention",
    )(lam_vecs.astype(F32), subln.reshape(1, DA_V_DIM).astype(F32), qk, qk, u, u)


def _gdn_prep_kernel(um, ul, ur, ab_ref, cw_ref, al_ref, dt_ref,
                     q_ref, k_ref, kt_ref, v_ref, gb_ref, gbt_ref, U_ref, *, tt, Tp, n_qk, n_g):
    row0 = pl.program_id(1) * tt

    def masked(ref, base, n):
        return jnp.where(_row_valid(base, n, Tp), ref[0].astype(F32), 0.0)

    U_ref[0:HALO] = masked(ul, row0 - HALO, HALO)
    U_ref[HALO:HALO + tt] = masked(um, row0, tt)
    U_ref[HALO + tt:HALO + tt + HALO] = masked(ur, row0 + tt, HALO)
    valid = _row_valid(row0, tt, Tp)

    def conv_silu(c0, width):
        cols = slice(c0, c0 + width)
        y = (cw_ref[0:1, cols] * U_ref[HALO - 1:HALO - 1 + tt, cols]
             + cw_ref[1:2, cols] * U_ref[HALO:HALO + tt, cols]
             + cw_ref[2:3, cols] * U_ref[HALO + 1:HALO + 1 + tt, cols])
        return _silu(y)

    def l2n(x):
        return x * lax.rsqrt(jnp.sum(x * x, axis=-1, keepdims=True) + 1e-6)

    for hh in range(n_qk // LANE):
        c0 = hh * LANE
        qh = l2n(conv_silu(c0, LANE)) * GDN_HEAD_DIM ** -0.5
        q_ref[0, :, c0:c0 + LANE] = jnp.where(valid, qh, 0.0).astype(q_ref.dtype)
        kh = jnp.where(valid, l2n(conv_silu(n_qk + c0, LANE)), 0.0)
        k_ref[0, :, c0:c0 + LANE] = kh.astype(k_ref.dtype)
        kt_ref[0, c0:c0 + LANE, :] = kh.T.astype(kt_ref.dtype)
    n_v = v_ref.shape[2]
    for c0 in range(0, n_v, LANE):
        vh = conv_silu(2 * n_qk + c0, LANE)
        v_ref[0, :, c0:c0 + LANE] = jnp.where(valid, vh, 0.0).astype(v_ref.dtype)

    x = ab_ref[0].astype(F32)
    lane = lax.broadcasted_iota(jnp.int32, (1, LANE), 1)
    xa = x + dt_ref[...]
    softplus = jnp.maximum(xa, 0.0) + jnp.log(1.0 + jnp.exp(-jnp.abs(xa)))
    g = -jnp.exp(al_ref[...]) * softplus
    beta = jax.nn.sigmoid(x)
    gb = jnp.where(lane < n_g, g, jnp.where(lane < 2 * n_g, beta, 0.0))
    gb = jnp.where(valid, gb, 0.0)

    H = n_g // 2
    C = GDN_CHUNK
    r = lax.broadcasted_iota(jnp.int32, (C, C), 0)
    c = lax.broadcasted_iota(jnp.int32, (C, C), 1)
    lower = (c <= r).astype(F32)
    upper = (c >= r).astype(F32)
    hi = lax.Precision.HIGHEST

    def pack(gc, bt, tot):
        return jnp.where(lane < H, gc, jnp.where(lane < 2 * H, bt, jnp.where(lane < 3 * H, tot, 0.0)))

    for c3 in range(tt // C):
        rows = slice(c3 * C, (c3 + 1) * C)
        gch = gb[rows]
        cf = jnp.dot(lower, gch, precision=hi, preferred_element_type=F32)
        cb = jnp.dot(upper, gch, precision=hi, preferred_element_type=F32)
        tot = jnp.broadcast_to(jnp.sum(gch, axis=0, keepdims=True), (C, LANE))
        g0 = pack(cf, pltpu.roll(gch, LANE - H, axis=1), pltpu.roll(tot, 2 * H, axis=1))
        g1 = pack(pltpu.roll(cb, LANE - H, axis=1), pltpu.roll(gch, LANE - 2 * H, axis=1),
                  pltpu.roll(tot, H, axis=1))
        gb_ref[0, 0, rows, :] = g0
        gb_ref[1, 0, rows, :] = g1
        gbt_ref[0, 0, :, rows] = g0.T
        gbt_ref[1, 0, :, rows] = g1.T


def _gdn_prep(u, conv_w, a_log, dt_bias, n_qk, E):
    B, Tp, _ = u.shape
    n_conv = 2 * n_qk + E
    n_g = a_log.size
    assert 3 * (n_g // 2) <= LANE
    tt = _tile(Tp, 384, GDN_CHUNK)
    specs = _halo_specs(n_conv, tt, Tp, lambda: 0)
    al = jnp.zeros((1, LANE), F32).at[0, :n_g].set(a_log.reshape(-1).astype(F32))
    dt = jnp.zeros((1, LANE), F32).at[0, :n_g].set(dt_bias.reshape(-1).astype(F32))
    vec = pl.BlockSpec((1, LANE), lambda b, i: (0, 0))
    return pl.pallas_call(
        functools.partial(_gdn_prep_kernel, tt=tt, Tp=Tp, n_qk=n_qk, n_g=n_g),
        grid=(B, Tp // tt),
        in_specs=[*specs,
                  pl.BlockSpec((1, tt, LANE), lambda b, i: (b, i, (n_conv + E) // LANE)),
                  pl.BlockSpec((HY_SHORT, n_conv), lambda b, i: (0, 0)),
                  vec, vec],
        out_specs=[
            pl.BlockSpec((1, tt, n_qk), lambda b, i: (b, i, 0)),
            pl.BlockSpec((1, tt, n_qk), lambda b, i: (b, i, 0)),
            pl.BlockSpec((1, n_qk, tt), lambda b, i: (b, 0, i)),
            pl.BlockSpec((1, tt, E), lambda b, i: (b, i, 0)),
            pl.BlockSpec((2, 1, tt, LANE), lambda b, i: (0, b, i, 0)),
            pl.BlockSpec((2, 1, LANE, tt), lambda b, i: (0, b, 0, i)),
        ],
        out_shape=[
            jax.ShapeDtypeStruct((B, Tp, n_qk), BF16),
            jax.ShapeDtypeStruct((B, Tp, n_qk), BF16),
            jax.ShapeDtypeStruct((B, n_qk, Tp), BF16),
            jax.ShapeDtypeStruct((B, Tp, E), BF16),
            jax.ShapeDtypeStruct((2, B, Tp, LANE), F32),
            jax.ShapeDtypeStruct((2, B, LANE, Tp), F32),
        ],
        scratch_shapes=[pltpu.VMEM((tt + 2 * HALO, n_conv), F32)],
        compiler_params=_params("parallel", "parallel"),
        name="gdn_prep",
    )(u, u, u, u, conv_w.astype(F32), al, dt)


def _mm(a, b):
    return jnp.dot(a.astype(BF16), b.astype(BF16), preferred_element_type=F32)


def _unit_lower_inverse(mats):
    C = mats[0].shape[0]
    row = lax.broadcasted_iota(jnp.int32, (C, C), 0)
    col = lax.broadcasted_iota(jnp.int32, (C, C), 1)
    rc = row ^ col
    ps = [jnp.where(rc < 8, a, 0.0) for a in mats]
    xs = [jnp.where(rc == 0, 1.0, 0.0) - p for p in ps]
    n = 2
    while n < 8:
        ps = [_mm(p, p) for p in ps]
        xs = [x + _mm(x, p) for x, p in zip(xs, ps)]
        n *= 2
    b = 8
    while b < C:
        sel = (rc >= b) & (rc < 2 * b)
        ys = [_mm(x, jnp.where(sel, a, 0.0)) for x, a in zip(xs, mats)]
        xs = [x - _mm(y, x) for x, y in zip(xs, ys)]
        b *= 2
    return xs


def _gdn_chunk_kernel(q_ref, k_ref, kt_ref, v_ref, gb_ref, gbt_ref, o_ref, S_ref, *, n_heads):
    d = pl.program_id(0)

    @pl.when(pl.program_id(2) == 0)
    def _():
        S_ref[...] = jnp.zeros_like(S_ref)

    C = q_ref.shape[1]
    H = n_heads
    rep = n_heads // (q_ref.shape[2] // GDN_HEAD_DIM)
    row = lax.broadcasted_iota(jnp.int32, (C, C), 0)
    col = lax.broadcasted_iota(jnp.int32, (C, C), 1)
    ahead = (row - col) * (1 - 2 * d)
    incl = ahead >= 0
    strict = ahead > 0
    gb = gb_ref[0, 0]
    gbt = gbt_ref[0, 0]

    heads = range(n_heads)
    kcols = [slice((hv // rep) * LANE, (hv // rep + 1) * LANE) for hv in heads]
    kk, qk = [], []
    for kh in range(n_heads // rep):
        cols = slice(kh * LANE, (kh + 1) * LANE)
        k = k_ref[0, :, cols]
        kk.append(lax.dot_general(k, k, _NT, preferred_element_type=F32))
        qk.append(lax.dot_general(q_ref[0, :, cols], k, _NT, preferred_element_type=F32))

    gc = [gb[:, hv:hv + 1] for hv in heads]
    beta = [gb[:, H + hv:H + hv + 1] for hv in heads]
    gc_row = [gbt[hv:hv + 1, :] for hv in heads]
    gtot_row = [gbt[2 * H + hv:2 * H + hv + 1, :] for hv in heads]
    decay = [jnp.exp(jnp.where(incl, gc[hv] - gc_row[hv], -1e30)) for hv in heads]
    a = [jnp.where(strict, beta[hv] * kk[hv // rep] * decay[hv], 0.0) for hv in heads]
    t_inv = _unit_lower_inverse(a)
    e_gc = [jnp.exp(g) for g in gc]
    rhs = [jnp.concatenate([v_ref[0, :, hv * LANE:(hv + 1) * LANE].astype(F32) * beta[hv],
                            k_ref[0, :, kcols[hv]].astype(F32) * (beta[hv] * e_gc[hv])], axis=1)
           for hv in heads]
    sol = [_mm(t_inv[hv], rhs[hv]) for hv in heads]
    s_old = [S_ref[hv] for hv in heads]
    r = [_mm(jnp.concatenate([sol[hv][:, LANE:], q_ref[0, :, kcols[hv]].astype(F32) * e_gc[hv]], axis=0),
             s_old[hv]) for hv in heads]
    v_new = [(sol[hv][:, :LANE] - r[hv][:C]).astype(BF16) for hv in heads]
    for hv in heads:
        o = r[hv][C:] + _mm(qk[hv // rep] * decay[hv], v_new[hv])
        o_ref[0, 0, :, hv * LANE:(hv + 1) * LANE] = o.astype(o_ref.dtype)
    for hv in heads:
        k_end_t = kt_ref[0, kcols[hv], :].astype(F32) * jnp.exp(gtot_row[hv] - gc_row[hv])
        S_ref[hv] = s_old[hv] * jnp.exp(gtot_row[hv][:, 0:1]) + _mm(k_end_t, v_new[hv])


def _gdn_scan(q, k, kt, v, gb, gbt):
    B, Tp, n_qk = q.shape
    E = v.shape[2]
    n_heads = E // GDN_HEAD_DIM
    C = GDN_CHUNK
    nC = Tp // C

    def cc(d, c):
        return c + d * (nC - 1 - 2 * c)

    return pl.pallas_call(
        functools.partial(_gdn_chunk_kernel, n_heads=n_heads),
        grid=(2, B, nC),
        in_specs=[
            pl.BlockSpec((1, C, n_qk), lambda d, b, c: (b, cc(d, c), 0)),
            pl.BlockSpec((1, C, n_qk), lambda d, b, c: (b, cc(d, c), 0)),
            pl.BlockSpec((1, n_qk, C), lambda d, b, c: (b, 0, cc(d, c))),
            pl.BlockSpec((1, C, E), lambda d, b, c: (b, cc(d, c), 0)),
            pl.BlockSpec((1, 1, C, LANE), lambda d, b, c: (d, b, cc(d, c), 0)),
            pl.BlockSpec((1, 1, LANE, C), lambda d, b, c: (d, b, 0, cc(d, c))),
        ],
        out_specs=pl.BlockSpec((1, 1, C, E), lambda d, b, c: (d, b, cc(d, c), 0)),
        out_shape=jax.ShapeDtypeStruct((2, B, Tp, E), BF16),
        scratch_shapes=[pltpu.VMEM((n_heads, GDN_HEAD_DIM, GDN_HEAD_DIM), F32)],
        compiler_params=_params("parallel", "parallel", "arbitrary"),
        name="gdn_scan",
    )(q, k, kt, v, gb, gbt)


def _gdn_out_kernel(of_ref, ob_ref, z_ref, on_ref, w_ref, g_ref, h_ref, o_ref):
    E = of_ref.shape[3]
    parts = []
    for c0 in range(0, E, LANE):
        o = of_ref[0, 0, :, c0:c0 + LANE].astype(F32) + ob_ref[0, 0, :, c0:c0 + LANE].astype(F32)
        ms = jnp.mean(o * o, axis=-1, keepdims=True)
        o = o * lax.rsqrt(ms + RMS_EPS) * on_ref[...]
        parts.append((o * _silu(z_ref[0, :, c0:c0 + LANE].astype(F32))).astype(BF16))
    a = jnp.concatenate(parts, axis=1)
    y = jnp.dot(a, w_ref[...], preferred_element_type=F32)
    _post_norm_residual(y, g_ref, h_ref, o_ref)


def _gdn_out(o2, u, z_col, o_norm, w, gain, h):
    B, Tp, D = h.shape
    E = o2.shape[3]
    tm = _tile(Tp, 528)
    return pl.pallas_call(
        _gdn_out_kernel,
        grid=(B, Tp // tm),
        in_specs=[
            pl.BlockSpec((1, 1, tm, E), lambda b, i: (0, b, i, 0)),
            pl.BlockSpec((1, 1, tm, E), lambda b, i: (1, b, i, 0)),
            pl.BlockSpec((1, tm, E), lambda b, i: (b, i, z_col)),
            pl.BlockSpec((1, GDN_HEAD_DIM), lambda b, i: (0, 0)),
            pl.BlockSpec((E, D), lambda b, i: (0, 0)),
            pl.BlockSpec((1, D), lambda b, i: (0, 0)),
            pl.BlockSpec((1, tm, D), lambda b, i: (b, i, 0)),
        ],
        out_specs=pl.BlockSpec((1, tm, D), lambda b, i: (b, i, 0)),
        out_shape=jax.ShapeDtypeStruct((B, Tp, D), F32),
        compiler_params=_params("parallel", "parallel"),
        name="gdn_out",
    )(o2, o2, u, o_norm.reshape(1, GDN_HEAD_DIM).astype(F32), w, gain.reshape(1, D).astype(F32), h)


def _hy_block(Tp):
    P = 768 if Tp >= 1536 else 128
    return P, -(-Tp // P)


def _dft_mats(P):
    k = np.arange(P, dtype=np.int64)[:, None]
    b = np.arange(P, dtype=np.int64)[None, :]

    def mat(m):
        ang = np.pi * (((2 * k + 1) * m) % (4 * P)).astype(np.float64) / (2 * P)
        return np.concatenate([np.cos(ang), -np.sin(ang)], axis=0).astype(np.float32)

    fwd = mat(b)
    lo = mat(b - P)
    as_bf16 = lambda a: jnp.asarray(a).astype(BF16)
    return as_bf16(fwd), as_bf16(lo), as_bf16(np.ascontiguousarray(fwd.T))


def _hy_filter_kernel(band_ref, w1_ref, b1_ref, w2_ref, b2_ref, w3_ref, b3_ref, fr_ref,
                      w4_ref, w4b_ref, dl_ref, g_ref, n_ref, *, P, J, T):
    i = pl.program_id(0)
    hi = lax.Precision.HIGHEST
    d = i * P + lax.broadcasted_iota(jnp.int32, (P, 1), 0) - J * P
    s = jnp.abs(d).astype(F32)
    t = s / (T - 1)
    w = (2.0 * math.pi) * s / T
    lane = lax.broadcasted_iota(jnp.int32, (1, LANE), 1)
    arg = band_ref[...] * w
    z = jnp.where(lane == 0, t,
                  jnp.where(lane <= HY_BANDS, jnp.cos(arg),
                            jnp.where(lane <= 2 * HY_BANDS, -jnp.sin(arg), 0.0)))
    fr = fr_ref[...]
    hdn = jnp.sin(fr * (jnp.dot(z, w1_ref[...], precision=hi, preferred_element_type=F32) + b1_ref[...]))
    hdn = jnp.sin(fr * (jnp.dot(hdn, w2_ref[...], precision=hi, preferred_element_type=F32) + b2_ref[...]))
    hdn = jnp.sin(fr * (jnp.dot(hdn, w3_ref[...], precision=hi, preferred_element_type=F32) + b3_ref[...]))
    filt = jnp.dot(hdn, w4_ref[...], precision=hi, preferred_element_type=F32) * jnp.exp(-t * dl_ref[...])
    g = jnp.where(s < T, filt, 0.0)
    g_ref[...] = g
    part = jnp.sum(jnp.abs(g), axis=0, keepdims=True)

    @pl.when(i == 0)
    def _():
        n_ref[...] = jnp.zeros_like(n_ref)

    n_ref[...] += part

    @pl.when(i == J)
    def _():
        extra = jnp.dot(hdn[0:8], w4b_ref[...], precision=hi, preferred_element_type=F32)
        n_ref[...] += jnp.abs(extra[0:1])


def _hy_filter(f_w1, f_b1, f_w2, f_b2, f_w3, f_b3, f_w4, f_freq, E, T, P, J):
    pad2 = lambda a: jnp.zeros((LANE, LANE), F32).at[:a.shape[0], :a.shape[1]].set(a.astype(F32))
    padv = lambda a: jnp.zeros((1, LANE), F32).at[0, :a.shape[0]].set(a.astype(F32))
    bands = jnp.linspace(1e-4, HY_BANDS - 1, HY_BANDS, dtype=F32)
    band = jnp.zeros((1, LANE), F32).at[0, 1:1 + HY_BANDS].set(bands).at[0, 1 + HY_BANDS:1 + 2 * HY_BANDS].set(bands)
    w4 = jnp.zeros((LANE, 2 * E), F32).at[:HY_HIDDEN].set(f_w4.astype(F32))
    max_decay = math.log(HY_DECAY_TARGET) / HY_SHORT_DECAY_PCT
    min_decay = math.log(HY_DECAY_TARGET) / HY_LONG_DECAY_PCT
    deltas = jnp.abs(jnp.linspace(min_decay, max_decay, E, dtype=F32)).reshape(1, E)
    sq = pl.BlockSpec((LANE, LANE), lambda i: (0, 0))
    vec = pl.BlockSpec((1, LANE), lambda i: (0, 0))
    return pl.pallas_call(
        functools.partial(_hy_filter_kernel, P=P, J=J, T=T),
        grid=(2 * J,),
        in_specs=[vec, sq, vec, sq, vec, sq, vec, vec,
                  pl.BlockSpec((LANE, E), lambda i: (0, jnp.where(i >= J, 0, 1))),
                  pl.BlockSpec((LANE, E), lambda i: (0, 1)),
                  pl.BlockSpec((1, E), lambda i: (0, 0))],
        out_specs=[pl.BlockSpec((P, E), lambda i: (i, 0)),
                   pl.BlockSpec((1, E), lambda i: (0, 0))],
        out_shape=[jax.ShapeDtypeStruct((2 * J * P, E), F32),
                   jax.ShapeDtypeStruct((1, E), F32)],
        compiler_params=_params("arbitrary"),
        name="hyena_filter",
    )(band, pad2(f_w1), padv(f_b1), pad2(f_w2), padv(f_b2), pad2(f_w3), padv(f_b3), padv(f_freq),
      w4, w4, deltas)


def _hy_spec_kernel(lo_ref, hi_ref, flo_ref, fhi_ref, n_ref, gr_ref, gi_ref, *, P):
    spec = (jnp.dot(flo_ref[...], lo_ref[...].astype(BF16), preferred_element_type=F32)
            + jnp.dot(fhi_ref[...], hi_ref[...].astype(BF16), preferred_element_type=F32))
    scale = 1.0 / ((n_ref[...] + HY_FILTER_EPS) * P)
    gr_ref[0] = spec[:P] * scale
    gi_ref[0] = spec[P:] * scale


def _hy_spectrum(g, norm, fwd, lo, P, J):
    E = g.shape[1]
    cs = _tile(E, 256, LANE)
    mat = pl.BlockSpec((2 * P, P), lambda c, q: (0, 0))
    out = pl.BlockSpec((1, P, cs), lambda c, q: (q, 0, c))
    shape = jax.ShapeDtypeStruct((2 * J - 1, P, E), F32)
    return pl.pallas_call(
        functools.partial(_hy_spec_kernel, P=P),
        grid=(E // cs, 2 * J - 1),
        in_specs=[pl.BlockSpec((P, cs), lambda c, q: (q, c)),
                  pl.BlockSpec((P, cs), lambda c, q: (q + 1, c)),
                  mat, mat,
                  pl.BlockSpec((1, cs), lambda c, q: (0, c))],
        out_specs=[out, out],
        out_shape=[shape, shape],
        compiler_params=_params("parallel", "parallel"),
        name="hyena_spectrum",
    )(g, g, lo, fwd, norm)


def _hy_prep_kernel(x0m, x0l, x0r, x1m, x1l, x1r, vm, vl, vr, z_ref, cw_ref, cb_ref,
                    w_ref, gg_ref, U_ref, *, tt, Tp, ce):
    row0 = pl.program_id(1) * tt
    valid = _row_valid(row0, tt, Tp)

    def conv(k, main, left, right):
        def masked(ref, base, n):
            return jnp.where(_row_valid(base, n, Tp), ref[0].astype(F32), 0.0)

        U_ref[0:HALO] = masked(left, row0 - HALO, HALO)
        U_ref[HALO:HALO + tt] = masked(main, row0, tt)
        U_ref[HALO + tt:HALO + tt + HALO] = masked(right, row0 + tt, HALO)
        cw = cw_ref[k]
        return (cw[0:1] * U_ref[HALO - 1:HALO - 1 + tt] + cw[1:2] * U_ref[HALO:HALO + tt]
                + cw[2:3] * U_ref[HALO + 1:HALO + 1 + tt] + cb_ref[k])

    x1 = conv(1, x1m, x1l, x1r)
    v = conv(2, vm, vl, vr)
    w_ref[0] = jnp.where(valid, v * x1, 0.0)
    x0 = conv(0, x0m, x0l, x0r)
    gg_ref[0] = jnp.where(valid, x0 * _silu(z_ref[0].astype(F32)), 0.0).astype(gg_ref.dtype)


def _hy_prep(u, conv_w, conv_b, E, TH):
    B, Tp, _ = u.shape
    ce = _tile(E, 512, LANE)
    nce = E // ce
    tt = _tile(math.gcd(Tp, TH), 384, HALO)
    specs = []
    for s in range(3):
        specs += _halo_specs(ce, tt, Tp, lambda c, s=s: s * nce + c)
    nt = Tp // tt
    cw = conv_w.astype(F32).reshape(HY_SHORT, 3, E).transpose(1, 0, 2)
    cb = conv_b.astype(F32).reshape(3, 1, E)
    return pl.pallas_call(
        functools.partial(_hy_prep_kernel, tt=tt, Tp=Tp, ce=ce),
        grid=(B, TH // tt, nce),
        in_specs=[*specs,
                  pl.BlockSpec((1, tt, ce), lambda b, i, c: (b, jnp.minimum(i, nt - 1), 3 * nce + c)),
                  pl.BlockSpec((3, HY_SHORT, ce), lambda b, i, c: (0, 0, c)),
                  pl.BlockSpec((3, 1, ce), lambda b, i, c: (0, 0, c))],
        out_specs=[pl.BlockSpec((1, tt, ce), lambda b, i, c: (b, i, c)),
                   pl.BlockSpec((1, tt, ce), lambda b, i, c: (b, i, c))],
        out_shape=[jax.ShapeDtypeStruct((B, TH, E), F32),
                   jax.ShapeDtypeStruct((B, TH, E), BF16)],
        scratch_shapes=[pltpu.VMEM((tt + 2 * HALO, ce), F32)],
        compiler_params=_params("parallel", "parallel", "parallel"),
        name="hyena_prep",
    )(*([u] * 10), cw, cb)


def _hy_conv_kernel(w_ref, gg_ref, fwd_ref, inv_ref, gr_ref, gi_ref, sk_ref, o_ref, yr_ref, yi_ref,
                    *, P, J):
    s = pl.program_id(2)
    nb = w_ref.shape[0]

    @pl.when(s == 0)
    def _():
        yr_ref[...] = jnp.zeros_like(yr_ref)
        yi_ref[...] = jnp.zeros_like(yi_ref)

    @pl.when(s < J)
    def _():
        wv = jnp.concatenate([w_ref[b] for b in range(nb)], axis=1).astype(BF16)
        spec = jnp.dot(fwd_ref[...], wv, preferred_element_type=F32)
        vr, vi = spec[:P], spec[P:]
        for i in range(J):
            q = i - s + J - 1
            gr = jnp.concatenate([gr_ref[q]] * nb, axis=1)
            gi = jnp.concatenate([gi_ref[q]] * nb, axis=1)
            yr_ref[i] += vr * gr - vi * gi
            yi_ref[i] += vr * gi + vi * gr

    @pl.when(s >= J)
    def _():
        i = s - J
        spec = jnp.concatenate([yr_ref[i], yi_ref[i]], axis=0).astype(BF16)
        y = jnp.dot(inv_ref[...], spec, preferred_element_type=F32)
        for b in range(nb):
            yb = y[:, b * LANE:(b + 1) * LANE] + sk_ref[...] * w_ref[b]
            o_ref[b] = (yb * gg_ref[b].astype(F32)).astype(o_ref.dtype)


def _hy_long_conv(w, gg, fwd, inv, gr, gi, skip, P, J):
    B, TH, E = w.shape
    nb = 2 if B % 2 == 0 else 1
    blk = lambda fn: pl.BlockSpec((nb, P, LANE), fn)
    spectra = pl.BlockSpec((2 * J - 1, P, LANE), lambda c, b, s: (0, 0, c))
    return pl.pallas_call(
        functools.partial(_hy_conv_kernel, P=P, J=J),
        grid=(E // LANE, B // nb, 2 * J),
        in_specs=[blk(lambda c, b, s: (b, jnp.where(s < J, s, s - J), c)),
                  blk(lambda c, b, s: (b, jnp.maximum(s - J, 0), c)),
                  pl.BlockSpec((2 * P, P), lambda c, b, s: (0, 0)),
                  pl.BlockSpec((P, 2 * P), lambda c, b, s: (0, 0)),
                  spectra, spectra,
                  pl.BlockSpec((1, LANE), lambda c, b, s: (0, c))],
        out_specs=blk(lambda c, b, s: (b, jnp.maximum(s - J, 0), c)),
        out_shape=jax.ShapeDtypeStruct((B, TH, E), BF16),
        scratch_shapes=[pltpu.VMEM((J, P, nb * LANE), F32), pltpu.VMEM((J, P, nb * LANE), F32)],
        compiler_params=_params("parallel", "parallel", "arbitrary"),
        name="hyena_long_conv",
    )(w, gg, fwd, inv, gr, gi, skip.reshape(1, E).astype(F32))


def _hyena_layer(h, gp, go, w_in, b_in, conv_w, conv_b, f_w1, f_b1, f_w2, f_b2, f_w3, f_b3, f_w4,
                 f_freq, skip, w_out):
    B, Tp, D = h.shape
    E = w_out.shape[0]
    T = Tp - FRONT
    P, J = _hy_block(Tp)
    u = _in_proj(h, gp, w_in.astype(BF16), b_in)
    g, norm = _hy_filter(f_w1, f_b1, f_w2, f_b2, f_w3, f_b3, f_w4, f_freq, E, T, P, J)
    fwd, lo, inv = _dft_mats(P)
    gr, gi = _hy_spectrum(g, norm, fwd, lo, P, J)
    w, gg = _hy_prep(u, conv_w, conv_b, E, J * P)
    a = _hy_long_conv(w, gg, fwd, inv, gr, gi, skip, P, J)
    return _out_proj(a, w_out.astype(BF16), jnp.zeros((D,), F32), go, h)


def _attention_layer(h, gp, go, w_in, lam_vecs, subln, w_out, layer_idx):
    B, Tp, D = h.shape
    E = w_out.shape[0]
    n_qk = (w_in.shape[1] - 2 * E) // 2
    u = _in_proj(h, gp, w_in.astype(BF16), jnp.zeros((w_in.shape[1],), F32))
    qk = _rope(u, n_qk, Tp)
    a = _diff_attention_core(u, qk, lam_vecs, subln, layer_idx, E)
    return _out_proj(a, w_out.astype(BF16), jnp.zeros((D,), F32), go, h)


def _gdn_layer(h, gp, go, w_in, conv_w, a_log, dt_bias, o_norm, w_out):
    B, Tp, D = h.shape
    E = w_out.shape[0]
    n_in = w_in.shape[1]
    n_conv = conv_w.shape[1]
    n_qk = (n_conv - E) // 2
    n_pad = -(-n_in // LANE) * LANE
    w_p = jnp.zeros((D, n_pad), BF16).at[:, :n_in].set(w_in.astype(BF16))
    u = _in_proj(h, gp, w_p, jnp.zeros((n_pad,), F32))
    q, k, kt, v, gb, gbt = _gdn_prep(u, conv_w, a_log, dt_bias, n_qk, E)
    o2 = _gdn_scan(q, k, kt, v, gb, gbt)
    return _gdn_out(o2, u, n_conv // E, o_norm, w_out.astype(BF16), go, h)


def _conformer_layer(h, gp, go, w_in, b_in, dw_w, dw_b, ln_g, ln_b, w_out, b_out):
    u = _in_proj(h, gp, w_in.astype(BF16), b_in)
    a = _conformer_core(u, dw_w, dw_b, ln_g, ln_b)
    return _out_proj(a, w_out.astype(BF16), b_out, go, h)


def kernel(x, meta, norm_pre, norm_post, hy_w_in, hy_b_in, hy_conv_w, hy_conv_b, hy_f_w1, hy_f_b1, hy_f_w2, hy_f_b2, hy_f_w3, hy_f_b3, hy_f_w4, hy_f_freq, hy_skip, hy_w_out, da_w_in, da_lambda, da_subln, da_w_out, gdn_w_in, gdn_conv_w, gdn_a_log, gdn_dt_bias, gdn_o_norm, gdn_w_out, cf_w_in, cf_b_in, cf_dw_w, cf_dw_b, cf_ln_g, cf_ln_b, cf_w_out, cf_b_out):
    B, S, D = x.shape
    assert S % LANE == 0
    h = jnp.concatenate([jnp.zeros((B, FRONT, D), F32),
                         jnp.broadcast_to(meta[None].astype(F32), (B, N_META, D)),
                         x.astype(F32)], axis=1)
    for i in range(norm_pre.shape[0]):
        m, j = i % 4, i // 4
        gp, go = norm_pre[i], norm_post[i]
        if m == 0:
            h = _hyena_layer(h, gp, go, hy_w_in[j], hy_b_in[j], hy_conv_w[j], hy_conv_b[j],
                             hy_f_w1[j], hy_f_b1[j], hy_f_w2[j], hy_f_b2[j], hy_f_w3[j], hy_f_b3[j],
                             hy_f_w4[j], hy_f_freq[j], hy_skip[j], hy_w_out[j])
        elif m == 1:
            h = _attention_layer(h, gp, go, da_w_in[j], da_lambda[j], da_subln[j], da_w_out[j], i)
        elif m == 2:
            h = _gdn_layer(h, gp, go, gdn_w_in[j], gdn_conv_w[j], gdn_a_log[j], gdn_dt_bias[j],
                           gdn_o_norm[j], gdn_w_out[j])
        else:
            h = _conformer_layer(h, gp, go, cf_w_in[j], cf_b_in[j], cf_dw_w[j], cf_dw_b[j],
                                 cf_ln_g[j], cf_ln_b[j], cf_w_out[j], cf_b_out[j])
    return h[:, FRONT + N_META:].astype(x.dtype)
```

```python
import functools
import math

import numpy as np
import jax
import jax.numpy as jnp
from jax import lax
from jax.experimental import pallas as pl
from jax.experimental.pallas import tpu as pltpu

F32 = jnp.float32
BF16 = jnp.bfloat16

N_META = 16
FRONT = 112
HALO = 16
LANE = 128
RMS_EPS = 1e-6
LN_EPS = 1e-5
ROPE_THETA = 10000.0
VMEM_LIMIT = 56 * 1024 * 1024

HY_SHORT = 3
HY_BANDS = 16
HY_HIDDEN = 64
HY_SHORT_DECAY_PCT = 0.3
HY_LONG_DECAY_PCT = 1.5
HY_DECAY_TARGET = 1e-2
HY_FILTER_EPS = 1e-6
DA_HEAD_DIM = 64
DA_V_DIM = 128
DA_NORM_EPS = 1e-5
ATTN_ANCHOR_ROWS = 16
GDN_HEAD_DIM = 128
GDN_CHUNK = 128
CF_WIDTH = 31

_NT = (((1,), (1,)), ((), ()))


def _tile(n, target, mult=8):
    best = None
    for t in range(mult, min(n, target) + 1, mult):
        if n % t == 0:
            best = t
    assert best is not None, (n, target, mult)
    return best


def _params(*sem):
    return pltpu.CompilerParams(dimension_semantics=sem, vmem_limit_bytes=VMEM_LIMIT)


def _sigmoid(x):
    return 0.5 * jnp.tanh(0.5 * x) + 0.5


def _silu(x):
    return x * _sigmoid(x)


def _row_valid(base, n, Tp):
    rows = base + lax.broadcasted_iota(jnp.int32, (n, 1), 0)
    return (rows >= FRONT) & (rows < Tp)


def _in_proj_kernel(h_ref, g_ref, w_ref, b_ref, o_ref):
    x = h_ref[0]
    ms = jnp.mean(x * x, axis=-1, keepdims=True)
    y = (x * lax.rsqrt(ms + RMS_EPS) * g_ref[...]).astype(BF16)
    acc = jnp.dot(y, w_ref[...], preferred_element_type=F32)
    o_ref[0] = (acc + b_ref[...]).astype(o_ref.dtype)


def _in_proj(h, gain, w, bias):
    B, Tp, D = h.shape
    N = w.shape[1]
    tm = _tile(Tp, 1056)
    tn = _tile(N, 2048, LANE)
    return pl.pallas_call(
        _in_proj_kernel,
        grid=(N // tn, B, Tp // tm),
        in_specs=[
            pl.BlockSpec((1, tm, D), lambda n, b, i: (b, i, 0)),
            pl.BlockSpec((1, D), lambda n, b, i: (0, 0)),
            pl.BlockSpec((D, tn), lambda n, b, i: (0, n)),
            pl.BlockSpec((1, tn), lambda n, b, i: (0, n)),
        ],
        out_specs=pl.BlockSpec((1, tm, tn), lambda n, b, i: (b, i, n)),
        out_shape=jax.ShapeDtypeStruct((B, Tp, N), BF16),
        compiler_params=_params("parallel", "parallel", "parallel"),
        name="in_proj",
    )(h, gain.reshape(1, D).astype(F32), w, bias.reshape(1, N).astype(F32))


def _post_norm_residual(y, g_ref, h_ref, o_ref):
    ms = jnp.mean(y * y, axis=-1, keepdims=True)
    o_ref[0] = h_ref[0] + y * lax.rsqrt(ms + RMS_EPS) * g_ref[...]


def _out_proj_kernel(a_ref, w_ref, b_ref, g_ref, h_ref, o_ref):
    y = jnp.dot(a_ref[0], w_ref[...], preferred_element_type=F32) + b_ref[...]
    _post_norm_residual(y, g_ref, h_ref, o_ref)


def _out_proj(a, w, bias, gain, h):
    B, Tp, D = h.shape
    E = a.shape[2]
    tm = _tile(Tp, 528)
    return pl.pallas_call(
        _out_proj_kernel,
        grid=(B, Tp // tm),
        in_specs=[
            pl.BlockSpec((1, tm, E), lambda b, i: (b, i, 0)),
            pl.BlockSpec((E, D), lambda b, i: (0, 0)),
            pl.BlockSpec((1, D), lambda b, i: (0, 0)),
            pl.BlockSpec((1, D), lambda b, i: (0, 0)),
            pl.BlockSpec((1, tm, D), lambda b, i: (b, i, 0)),
        ],
        out_specs=pl.BlockSpec((1, tm, D), lambda b, i: (b, i, 0)),
        out_shape=jax.ShapeDtypeStruct((B, Tp, D), F32),
        compiler_params=_params("parallel", "parallel"),
        name="out_proj",
    )(a, w, bias.reshape(1, D).astype(F32), gain.reshape(1, D).astype(F32), h)


def _halo_specs(width, tt, Tp, col):
    nt, nh, r = Tp // tt, Tp // HALO, tt // HALO
    main = pl.BlockSpec((1, tt, width), lambda b, i, *_: (b, jnp.minimum(i, nt - 1), col(*_)))
    left = pl.BlockSpec((1, HALO, width),
                        lambda b, i, *_: (b, jnp.clip(i * r - 1, 0, nh - 1), col(*_)))
    right = pl.BlockSpec((1, HALO, width),
                         lambda b, i, *_: (b, jnp.clip((i + 1) * r, 0, nh - 1), col(*_)))
    return main, left, right


CF_ROWS = 64


def _cf_kernel(am, al, ar, gm, gl, gr, z_ref, w_ref, b_ref, lg_ref, lb_ref, o_ref, G_ref, Y_ref,
               *, tt, Tp, E):
    row0 = pl.program_id(1) * tt

    def glu(a_ref, g_ref, base, n):
        a = a_ref[0].astype(F32)
        g = g_ref[0].astype(F32)
        return jnp.where(_row_valid(base, n, Tp), a * _sigmoid(g), 0.0)

    G_ref[0:HALO] = glu(al, gl, row0 - HALO, HALO)
    G_ref[HALO:HALO + tt] = glu(am, gm, row0, tt)
    G_ref[HALO + tt:HALO + tt + HALO] = glu(ar, gr, row0 + tt, HALO)

    shift = HALO - CF_WIDTH // 2
    SUB = 8

    def lane_body(c, carry):
        off = pl.multiple_of(c * LANE, LANE)
        lanes = pl.ds(off, LANE)
        bias = b_ref[:, lanes]
        taps = [w_ref[k:k + 1, lanes] for k in range(CF_WIDTH)]
        for rb in range(tt // CF_ROWS):
            acc = None
            for res in range(SUB):
                part = None
                for k in range(CF_WIDTH):
                    if (k + shift) % SUB != res:
                        continue
                    base = rb * CF_ROWS + (k + shift) - res
                    term = taps[k] * G_ref[pl.ds(base, CF_ROWS + SUB), lanes]
                    part = term if part is None else part + term
                if part is None:
                    continue
                part = part[res:res + CF_ROWS]
                acc = part if acc is None else acc + part
            Y_ref[pl.ds(rb * CF_ROWS, CF_ROWS), lanes] = acc + bias
        return carry

    lax.fori_loop(0, E // LANE, lane_body, 0)

    y = Y_ref[...]
    mu = jnp.mean(y, axis=-1, keepdims=True)
    yc = y - mu
    var = jnp.mean(yc * yc, axis=-1, keepdims=True)
    yn = yc * lax.rsqrt(var + LN_EPS) * lg_ref[...] + lb_ref[...]
    o_ref[0] = (_silu(yn) * _silu(z_ref[0].astype(F32))).astype(o_ref.dtype)


def _conformer_core(u, dw_w, dw_b, ln_g, ln_b):
    B, Tp, E3 = u.shape
    E = E3 // 3
    tt = _tile(Tp, 384, CF_ROWS)
    a_specs = _halo_specs(E, tt, Tp, lambda: 0)
    g_specs = _halo_specs(E, tt, Tp, lambda: 1)
    vec = pl.BlockSpec((1, E), lambda b, i: (0, 0))
    return pl.pallas_call(
        functools.partial(_cf_kernel, tt=tt, Tp=Tp, E=E),
        grid=(B, Tp // tt),
        in_specs=[*a_specs, *g_specs,
                  pl.BlockSpec((1, tt, E), lambda b, i: (b, i, 2)),
                  pl.BlockSpec((CF_WIDTH, E), lambda b, i: (0, 0)),
                  vec, vec, vec],
        out_specs=pl.BlockSpec((1, tt, E), lambda b, i: (b, i, 0)),
        out_shape=jax.ShapeDtypeStruct((B, Tp, E), BF16),
        scratch_shapes=[pltpu.VMEM((tt + 2 * HALO, E), F32), pltpu.VMEM((tt, E), F32)],
        compiler_params=_params("parallel", "parallel"),
        name="conformer_conv",
    )(u, u, u, u, u, u, u, dw_w.astype(F32), dw_b.reshape(1, E).astype(F32),
      ln_g.reshape(1, E).astype(F32), ln_b.reshape(1, E).astype(F32))


def _rope_kernel(u_ref, c_ref, s_ref, o_ref, *, n_q, scale):
    cos = c_ref[...]
    sin = s_ref[...]
    lane = lax.broadcasted_iota(jnp.int32, (1, LANE), 1)
    first = (lane % DA_HEAD_DIM) < DA_HEAD_DIM // 2
    for g in range(u_ref.shape[2] // LANE):
        x = u_ref[0, :, g * LANE:(g + 1) * LANE].astype(F32)
        partner = jnp.where(first, pltpu.roll(x, LANE - DA_HEAD_DIM // 2, axis=1),
                            pltpu.roll(x, DA_HEAD_DIM // 2, axis=1))
        r = x * cos + partner * sin
        if g < n_q:
            r = r * scale
        o_ref[0, :, g * LANE:(g + 1) * LANE] = r.astype(o_ref.dtype)


def _rope(u, n_qk, Tp):
    B = u.shape[0]
    half = DA_HEAD_DIM // 2
    inv_freq = ROPE_THETA ** (-jnp.arange(0, DA_HEAD_DIM, 2, dtype=F32) / DA_HEAD_DIM)
    pos = (jnp.arange(Tp, dtype=jnp.int32) - FRONT).astype(F32)
    ang = pos[:, None] * inv_freq[None, :]
    cos = jnp.tile(jnp.cos(ang), (1, LANE // half))
    sin = jnp.tile(jnp.concatenate([-jnp.sin(ang), jnp.sin(ang)], axis=1), (1, LANE // DA_HEAD_DIM))
    tm = _tile(Tp, 384)
    return pl.pallas_call(
        functools.partial(_rope_kernel, n_q=n_qk // LANE, scale=DA_HEAD_DIM ** -0.5 * math.log2(math.e)),
        grid=(B, Tp // tm),
        in_specs=[pl.BlockSpec((1, tm, 2 * n_qk), lambda b, i: (b, i, 0)),
                  pl.BlockSpec((tm, LANE), lambda b, i: (i, 0)),
                  pl.BlockSpec((tm, LANE), lambda b, i: (i, 0))],
        out_specs=pl.BlockSpec((1, tm, 2 * n_qk), lambda b, i: (b, i, 0)),
        out_shape=jax.ShapeDtypeStruct((B, Tp, 2 * n_qk), BF16),
        compiler_params=_params("parallel", "parallel"),
        name="rope",
    )(u, cos, sin)


def _attn_kernel(lv_ref, sub_ref, az_ref, q_ref, k_ref, v_ref, z_ref, o_ref, vx_ref, s_ref,
                 *, lam_init, Tp, kc, n_sub):
    first = FRONT + N_META

    @pl.when(pl.program_id(2) == 0)
    def _():
        vx_ref[:, :LANE] = v_ref[0]
        vx_ref[:, LANE:] = jnp.ones((Tp, LANE), BF16)

    lane = lax.broadcasted_iota(jnp.int32, (1, LANE), 1)
    chunks = [(0, first)] + [(r0, kc) for r0 in range(first, Tp, kc)]
    kvalid = lax.broadcasted_iota(jnp.int32, (1, first), 1) >= FRONT
    nc = len(chunks)
    n_acc = 2
    lv = lv_ref[...]
    lam = (jnp.exp(jnp.sum(lv[0:1] * lv[1:2], axis=-1, keepdims=True))
           - jnp.exp(jnp.sum(lv[2:3] * lv[3:4], axis=-1, keepdims=True)) + lam_init)

    def row_max(first_chunk, part):
        return jnp.maximum(jnp.max(part, axis=-1, keepdims=True),
                           jnp.max(first_chunk, axis=-1, keepdims=True))

    def accumulate(accs, t, c):
        accs[c % n_acc] = t if accs[c % n_acc] is None else accs[c % n_acc] + t

    tq = q_ref.shape[1]
    ts = tq // n_sub
    anchor = {}
    qmask = {}
    stats = {}

    def masked_q(t):
        if t not in qmask:
            q = q_ref[0, t * ts:(t + 1) * ts, :]
            if t - 1 in anchor:
                q = q + jnp.tile(anchor[t - 1], (ts // ATTN_ANCHOR_ROWS, 1))
            zero = jnp.zeros_like(q)
            qmask[t] = [jnp.where(lane < DA_HEAD_DIM, q, zero), jnp.where(lane >= DA_HEAD_DIM, q, zero)]
        return qmask[t]

    def scores(t, m, c):
        r0, n = chunks[c]
        s = lax.dot_general(masked_q(t)[m], k_ref[0, r0:r0 + n, :], _NT, preferred_element_type=F32)
        st = stats.setdefault((t, m), [None, None])
        if c == 0:
            s = jnp.where(kvalid, s, -1e30)
            st[0] = s
        else:
            st[1] = s if st[1] is None else jnp.maximum(st[1], s)
        s_ref[2 * (t % 2) + m, :, r0:r0 + n] = s

    def weighted(t, m, mx, c):
        r0, n = chunks[c]
        p = jnp.exp2((s_ref[2 * (t % 2) + m, :, r0:r0 + n] - mx).astype(BF16))
        if m == 0 and c == 0:
            anchor[t] = p[0:ATTN_ANCHOR_ROWS, 0:LANE] * az_ref[...]
        return jnp.dot(p, vx_ref[r0:r0 + n, :], preferred_element_type=F32)

    for m in range(2):
        for c in range(nc):
            scores(0, m, c)
    for t in range(n_sub):
        rows = slice(t * ts, (t + 1) * ts)
        outs = []
        for m in range(2):
            mx = row_max(*stats[(t, m)])
            accs = [None] * n_acc
            for c in range(nc):
                accumulate(accs, weighted(t, m, mx, c), c)
                if t + 1 < n_sub:
                    scores(t + 1, m, c)
            acc = functools.reduce(lambda x, y: x + y, accs)
            outs.append(acc[:, :LANE] / acc[:, LANE:LANE + 1])
        o = outs[0] - lam * outs[1]
        ms = jnp.mean(o * o, axis=-1, keepdims=True)
        o = o * lax.rsqrt(ms + DA_NORM_EPS) * sub_ref[...] * (1.0 - lam_init)
        o_ref[0, rows, :] = (o * _silu(z_ref[0, rows, :].astype(F32))).astype(o_ref.dtype)


def _diff_attention_core(u, qk, lam_vecs, subln, layer_idx, E):
    B, Tp, _ = u.shape
    H = E // DA_V_DIM
    tq = _tile(Tp, 1056)
    kc = 2 * LANE
    assert (Tp - FRONT - N_META) % kc == 0
    lam_init = 0.8 - 0.6 * math.exp(-0.3 * layer_idx)
    n_sub = max(n for n in range(1, 9) if tq % (ATTN_ANCHOR_ROWS * n) == 0)
    anchor_zeros = jnp.zeros((ATTN_ANCHOR_ROWS, LANE), BF16)
    return pl.pallas_call(
        functools.partial(_attn_kernel, lam_init=lam_init, Tp=Tp, kc=kc, n_sub=n_sub),
        grid=(B, H, Tp // tq),
        in_specs=[
            pl.BlockSpec((4, DA_HEAD_DIM), lambda b, h, i: (0, 0)),
            pl.BlockSpec((1, DA_V_DIM), lambda b, h, i: (0, 0)),
            pl.BlockSpec((ATTN_ANCHOR_ROWS, LANE), lambda b, h, i: (0, 0)),
            pl.BlockSpec((1, tq, LANE), lambda b, h, i: (b, i, h)),
            pl.BlockSpec((1, Tp, LANE), lambda b, h, i: (b, 0, H + h)),
            pl.BlockSpec((1, Tp, LANE), lambda b, h, i: (b, 0, 2 * H + h)),
            pl.BlockSpec((1, tq, LANE), lambda b, h, i: (b, i, 3 * H + h)),
        ],
        out_specs=pl.BlockSpec((1, tq, LANE), lambda b, h, i: (b, i, h)),
        out_shape=jax.ShapeDtypeStruct((B, Tp, E), BF16),
        scratch_shapes=[pltpu.VMEM((Tp, 2 * LANE), BF16), pltpu.VMEM((4, tq // n_sub, Tp), F32)],
        compiler_params=_params("parallel", "parallel", "arbitrary"),
        name="diff_attention",
    )(lam_vecs.astype(F32), subln.reshape(1, DA_V_DIM).astype(F32), anchor_zeros, qk, qk, u, u)


def _gdn_prep_kernel(um, ul, ur, ab_ref, cw_ref, al_ref, dt_ref,
                     q_ref, k_ref, kt_ref, v_ref, gb_ref, gbt_ref, U_ref, *, tt, Tp, n_qk, n_g):
    row0 = pl.program_id(1) * tt

    def masked(ref, base, n):
        return jnp.where(_row_valid(base, n, Tp), ref[0].astype(F32), 0.0)

    U_ref[0:HALO] = masked(ul, row0 - HALO, HALO)
    U_ref[HALO:HALO + tt] = masked(um, row0, tt)
    U_ref[HALO + tt:HALO + tt + HALO] = masked(ur, row0 + tt, HALO)
    valid = _row_valid(row0, tt, Tp)

    def conv_silu(c0, width):
        cols = slice(c0, c0 + width)
        y = (cw_ref[0:1, cols] * U_ref[HALO - 1:HALO - 1 + tt, cols]
             + cw_ref[1:2, cols] * U_ref[HALO:HALO + tt, cols]
             + cw_ref[2:3, cols] * U_ref[HALO + 1:HALO + 1 + tt, cols])
        return _silu(y)

    def l2n(x):
        return x * lax.rsqrt(jnp.sum(x * x, axis=-1, keepdims=True) + 1e-6)

    for hh in range(n_qk // LANE):
        c0 = hh * LANE
        qh = l2n(conv_silu(c0, LANE)) * GDN_HEAD_DIM ** -0.5
        q_ref[0, :, c0:c0 + LANE] = jnp.where(valid, qh, 0.0).astype(q_ref.dtype)
        kh = jnp.where(valid, l2n(conv_silu(n_qk + c0, LANE)), 0.0)
        k_ref[0, :, c0:c0 + LANE] = kh.astype(k_ref.dtype)
        kt_ref[0, c0:c0 + LANE, :] = kh.T.astype(kt_ref.dtype)
    n_v = v_ref.shape[2]
    for c0 in range(0, n_v, LANE):
        vh = conv_silu(2 * n_qk + c0, LANE)
        v_ref[0, :, c0:c0 + LANE] = jnp.where(valid, vh, 0.0).astype(v_ref.dtype)

    x = ab_ref[0].astype(F32)
    lane = lax.broadcasted_iota(jnp.int32, (1, LANE), 1)
    xa = x + dt_ref[...]
    softplus = jnp.maximum(xa, 0.0) + jnp.log(1.0 + jnp.exp(-jnp.abs(xa)))
    g = -jnp.exp(al_ref[...]) * softplus
    beta = _sigmoid(x)
    gb = jnp.where(lane < n_g, g, jnp.where(lane < 2 * n_g, beta, 0.0))
    gb = jnp.where(valid, gb, 0.0)

    H = n_g // 2
    C = GDN_CHUNK
    r = lax.broadcasted_iota(jnp.int32, (C, C), 0)
    c = lax.broadcasted_iota(jnp.int32, (C, C), 1)
    lower = (c <= r).astype(F32)
    upper = (c >= r).astype(F32)
    hi = lax.Precision.HIGHEST

    def pack(gc, bt, tot):
        return jnp.where(lane < H, gc, jnp.where(lane < 2 * H, bt, jnp.where(lane < 3 * H, tot, 0.0)))

    for c3 in range(tt // C):
        rows = slice(c3 * C, (c3 + 1) * C)
        gch = gb[rows]
        cf = jnp.dot(lower, gch, precision=hi, preferred_element_type=F32)
        cb = jnp.dot(upper, gch, precision=hi, preferred_element_type=F32)
        tot = jnp.broadcast_to(jnp.sum(gch, axis=0, keepdims=True), (C, LANE))
        g0 = pack(cf, pltpu.roll(gch, LANE - H, axis=1), pltpu.roll(tot, 2 * H, axis=1))
        g1 = pack(pltpu.roll(cb, LANE - H, axis=1), pltpu.roll(gch, LANE - 2 * H, axis=1),
                  pltpu.roll(tot, H, axis=1))
        gb_ref[0, 0, rows, :] = g0
        gb_ref[1, 0, rows, :] = g1
        gbt_ref[0, 0, :, rows] = g0.T
        gbt_ref[1, 0, :, rows] = g1.T


def _gdn_prep(u, conv_w, a_log, dt_bias, n_qk, E):
    B, Tp, _ = u.shape
    n_conv = 2 * n_qk + E
    n_g = a_log.size
    assert 3 * (n_g // 2) <= LANE
    tt = _tile(Tp, 384, GDN_CHUNK)
    specs = _halo_specs(n_conv, tt, Tp, lambda: 0)
    al = jnp.zeros((1, LANE), F32).at[0, :n_g].set(a_log.reshape(-1).astype(F32))
    dt = jnp.zeros((1, LANE), F32).at[0, :n_g].set(dt_bias.reshape(-1).astype(F32))
    vec = pl.BlockSpec((1, LANE), lambda b, i: (0, 0))
    return pl.pallas_call(
        functools.partial(_gdn_prep_kernel, tt=tt, Tp=Tp, n_qk=n_qk, n_g=n_g),
        grid=(B, Tp // tt),
        in_specs=[*specs,
                  pl.BlockSpec((1, tt, LANE), lambda b, i: (b, i, (n_conv + E) // LANE)),
                  pl.BlockSpec((HY_SHORT, n_conv), lambda b, i: (0, 0)),
                  vec, vec],
        out_specs=[
            pl.BlockSpec((1, tt, n_qk), lambda b, i: (b, i, 0)),
            pl.BlockSpec((1, tt, n_qk), lambda b, i: (b, i, 0)),
            pl.BlockSpec((1, n_qk, tt), lambda b, i: (b, 0, i)),
            pl.BlockSpec((1, tt, E), lambda b, i: (b, i, 0)),
            pl.BlockSpec((2, 1, tt, LANE), lambda b, i: (0, b, i, 0)),
            pl.BlockSpec((2, 1, LANE, tt), lambda b, i: (0, b, 0, i)),
        ],
        out_shape=[
            jax.ShapeDtypeStruct((B, Tp, n_qk), BF16),
            jax.ShapeDtypeStruct((B, Tp, n_qk), BF16),
            jax.ShapeDtypeStruct((B, n_qk, Tp), BF16),
            jax.ShapeDtypeStruct((B, Tp, E), BF16),
            jax.ShapeDtypeStruct((2, B, Tp, LANE), F32),
            jax.ShapeDtypeStruct((2, B, LANE, Tp), F32),
        ],
        scratch_shapes=[pltpu.VMEM((tt + 2 * HALO, n_conv), F32)],
        compiler_params=_params("parallel", "parallel"),
        name="gdn_prep",
    )(u, u, u, u, conv_w.astype(F32), al, dt)


def _mm(a, b):
    return jnp.dot(a.astype(BF16), b.astype(BF16), preferred_element_type=F32)


def _unit_lower_inverse(mats):
    C = mats[0].shape[0]
    row = lax.broadcasted_iota(jnp.int32, (C, C), 0)
    col = lax.broadcasted_iota(jnp.int32, (C, C), 1)
    rc = row ^ col
    ps = [jnp.where(rc < 8, a, 0.0) for a in mats]
    xs = [jnp.where(rc == 0, 1.0, 0.0) - p for p in ps]
    n = 2
    while n < 8:
        ps = [_mm(p, p) for p in ps]
        xs = [x + _mm(x, p) for x, p in zip(xs, ps)]
        n *= 2
    b = 8
    while b < C:
        sel = (rc >= b) & (rc < 2 * b)
        ys = [_mm(x, jnp.where(sel, a, 0.0)) for x, a in zip(xs, mats)]
        xs = [x - _mm(y, x) for x, y in zip(xs, ys)]
        b *= 2
    return xs


def _gdn_chunk_kernel(q_ref, k_ref, kt_ref, v_ref, gb_ref, gbt_ref, o_ref, S_ref, *, n_heads):
    d = pl.program_id(0)

    @pl.when(pl.program_id(2) == 0)
    def _():
        S_ref[...] = jnp.zeros_like(S_ref)

    C = q_ref.shape[1]
    H = n_heads
    rep = n_heads // (q_ref.shape[2] // GDN_HEAD_DIM)
    row = lax.broadcasted_iota(jnp.int32, (C, C), 0)
    col = lax.broadcasted_iota(jnp.int32, (C, C), 1)
    ahead = (row - col) * (1 - 2 * d)
    incl = ahead >= 0
    strict = ahead > 0
    gb = gb_ref[0, 0]
    gbt = gbt_ref[0, 0]

    heads = range(n_heads)
    kcols = [slice((hv // rep) * LANE, (hv // rep + 1) * LANE) for hv in heads]
    kk, qk = [], []
    for kh in range(n_heads // rep):
        cols = slice(kh * LANE, (kh + 1) * LANE)
        k = k_ref[0, :, cols]
        kk.append(lax.dot_general(k, k, _NT, preferred_element_type=F32))
        qk.append(lax.dot_general(q_ref[0, :, cols], k, _NT, preferred_element_type=F32))

    gc = [gb[:, hv:hv + 1] for hv in heads]
    beta = [gb[:, H + hv:H + hv + 1] for hv in heads]
    gc_row = [gbt[hv:hv + 1, :] for hv in heads]
    gtot_row = [gbt[2 * H + hv:2 * H + hv + 1, :] for hv in heads]
    decay = [jnp.exp(jnp.where(incl, gc[hv] - gc_row[hv], -1e30)) for hv in heads]
    a = [jnp.where(strict, beta[hv] * kk[hv // rep] * decay[hv], 0.0) for hv in heads]
    t_inv = _unit_lower_inverse(a)
    e_gc = [jnp.exp(g) for g in gc]
    rhs = [jnp.concatenate([v_ref[0, :, hv * LANE:(hv + 1) * LANE].astype(F32) * beta[hv],
                            k_ref[0, :, kcols[hv]].astype(F32) * (beta[hv] * e_gc[hv])], axis=1)
           for hv in heads]
    sol = [_mm(t_inv[hv], rhs[hv]) for hv in heads]
    s_old = [S_ref[hv] for hv in heads]
    r = [_mm(jnp.concatenate([sol[hv][:, LANE:], q_ref[0, :, kcols[hv]].astype(F32) * e_gc[hv]], axis=0),
             s_old[hv]) for hv in heads]
    v_new = [(sol[hv][:, :LANE] - r[hv][:C]).astype(BF16) for hv in heads]
    for hv in heads:
        o = r[hv][C:] + _mm(qk[hv // rep] * decay[hv], v_new[hv])
        o_ref[0, 0, :, hv * LANE:(hv + 1) * LANE] = o.astype(o_ref.dtype)
    for hv in heads:
        k_end_t = kt_ref[0, kcols[hv], :].astype(F32) * jnp.exp(gtot_row[hv] - gc_row[hv])
        S_ref[hv] = s_old[hv] * jnp.exp(gtot_row[hv][:, 0:1]) + _mm(k_end_t, v_new[hv])


def _gdn_scan(q, k, kt, v, gb, gbt):
    B, Tp, n_qk = q.shape
    E = v.shape[2]
    n_heads = E // GDN_HEAD_DIM
    C = GDN_CHUNK
    nC = Tp // C

    def cc(d, c):
        return c + d * (nC - 1 - 2 * c)

    return pl.pallas_call(
        functools.partial(_gdn_chunk_kernel, n_heads=n_heads),
        grid=(2, B, nC),
        in_specs=[
            pl.BlockSpec((1, C, n_qk), lambda d, b, c: (b, cc(d, c), 0)),
            pl.BlockSpec((1, C, n_qk), lambda d, b, c: (b, cc(d, c), 0)),
            pl.BlockSpec((1, n_qk, C), lambda d, b, c: (b, 0, cc(d, c))),
            pl.BlockSpec((1, C, E), lambda d, b, c: (b, cc(d, c), 0)),
            pl.BlockSpec((1, 1, C, LANE), lambda d, b, c: (d, b, cc(d, c), 0)),
            pl.BlockSpec((1, 1, LANE, C), lambda d, b, c: (d, b, 0, cc(d, c))),
        ],
        out_specs=pl.BlockSpec((1, 1, C, E), lambda d, b, c: (d, b, cc(d, c), 0)),
        out_shape=jax.ShapeDtypeStruct((2, B, Tp, E), BF16),
        scratch_shapes=[pltpu.VMEM((n_heads, GDN_HEAD_DIM, GDN_HEAD_DIM), F32)],
        compiler_params=_params("parallel", "parallel", "arbitrary"),
        name="gdn_scan",
    )(q, k, kt, v, gb, gbt)


def _gdn_out_kernel(of_ref, ob_ref, z_ref, on_ref, w_ref, g_ref, h_ref, o_ref):
    E = of_ref.shape[3]
    parts = []
    for c0 in range(0, E, LANE):
        o = of_ref[0, 0, :, c0:c0 + LANE].astype(F32) + ob_ref[0, 0, :, c0:c0 + LANE].astype(F32)
        ms = jnp.mean(o * o, axis=-1, keepdims=True)
        o = o * lax.rsqrt(ms + RMS_EPS) * on_ref[...]
        parts.append((o * _silu(z_ref[0, :, c0:c0 + LANE].astype(F32))).astype(BF16))
    a = jnp.concatenate(parts, axis=1)
    y = jnp.dot(a, w_ref[...], preferred_element_type=F32)
    _post_norm_residual(y, g_ref, h_ref, o_ref)


def _gdn_out(o2, u, z_col, o_norm, w, gain, h):
    B, Tp, D = h.shape
    E = o2.shape[3]
    tm = _tile(Tp, 528)
    return pl.pallas_call(
        _gdn_out_kernel,
        grid=(B, Tp // tm),
        in_specs=[
            pl.BlockSpec((1, 1, tm, E), lambda b, i: (0, b, i, 0)),
            pl.BlockSpec((1, 1, tm, E), lambda b, i: (1, b, i, 0)),
            pl.BlockSpec((1, tm, E), lambda b, i: (b, i, z_col)),
            pl.BlockSpec((1, GDN_HEAD_DIM), lambda b, i: (0, 0)),
            pl.BlockSpec((E, D), lambda b, i: (0, 0)),
            pl.BlockSpec((1, D), lambda b, i: (0, 0)),
            pl.BlockSpec((1, tm, D), lambda b, i: (b, i, 0)),
        ],
        out_specs=pl.BlockSpec((1, tm, D), lambda b, i: (b, i, 0)),
        out_shape=jax.ShapeDtypeStruct((B, Tp, D), F32),
        compiler_params=_params("parallel", "parallel"),
        name="gdn_out",
    )(o2, o2, u, o_norm.reshape(1, GDN_HEAD_DIM).astype(F32), w, gain.reshape(1, D).astype(F32), h)


def _hy_block(Tp):
    P = 768 if Tp >= 1536 else 128
    return P, -(-Tp // P)


def _dft_mats(P):
    k = np.arange(P, dtype=np.int64)[:, None]
    b = np.arange(P, dtype=np.int64)[None, :]

    def mat(m):
        ang = np.pi * (((2 * k + 1) * m) % (4 * P)).astype(np.float64) / (2 * P)
        return np.concatenate([np.cos(ang), -np.sin(ang)], axis=0).astype(np.float32)

    fwd = mat(b)
    lo = mat(b - P)
    as_bf16 = lambda a: jnp.asarray(a).astype(BF16)
    return as_bf16(fwd), as_bf16(lo), as_bf16(np.ascontiguousarray(fwd.T))


def _hy_filter_kernel(band_ref, w1_ref, b1_ref, w2_ref, b2_ref, w3_ref, b3_ref, fr_ref,
                      w4_ref, w4b_ref, dl_ref, g_ref, n_ref, *, P, J, T):
    i = pl.program_id(0)
    hi = lax.Precision.HIGHEST
    d = i * P + lax.broadcasted_iota(jnp.int32, (P, 1), 0) - J * P
    s = jnp.abs(d).astype(F32)
    t = s / (T - 1)
    w = (2.0 * math.pi) * s / T
    lane = lax.broadcasted_iota(jnp.int32, (1, LANE), 1)
    arg = band_ref[...] * w
    z = jnp.where(lane == 0, t,
                  jnp.where(lane <= HY_BANDS, jnp.cos(arg),
                            jnp.where(lane <= 2 * HY_BANDS, -jnp.sin(arg), 0.0)))
    fr = fr_ref[...]
    hdn = jnp.sin(fr * (jnp.dot(z, w1_ref[...], precision=hi, preferred_element_type=F32) + b1_ref[...]))
    hdn = jnp.sin(fr * (jnp.dot(hdn, w2_ref[...], precision=hi, preferred_element_type=F32) + b2_ref[...]))
    hdn = jnp.sin(fr * (jnp.dot(hdn, w3_ref[...], precision=hi, preferred_element_type=F32) + b3_ref[...]))
    filt = jnp.dot(hdn, w4_ref[...], precision=hi, preferred_element_type=F32) * jnp.exp(-t * dl_ref[...])
    g = jnp.where(s < T, filt, 0.0)
    g_ref[...] = g
    part = jnp.sum(jnp.abs(g), axis=0, keepdims=True)

    @pl.when(i == 0)
    def _():
        n_ref[...] = jnp.zeros_like(n_ref)

    n_ref[...] += part

    @pl.when(i == J)
    def _():
        extra = jnp.dot(hdn[0:8], w4b_ref[...], precision=hi, preferred_element_type=F32)
        n_ref[...] += jnp.abs(extra[0:1])


def _hy_filter(f_w1, f_b1, f_w2, f_b2, f_w3, f_b3, f_w4, f_freq, E, T, P, J):
    pad2 = lambda a: jnp.zeros((LANE, LANE), F32).at[:a.shape[0], :a.shape[1]].set(a.astype(F32))
    padv = lambda a: jnp.zeros((1, LANE), F32).at[0, :a.shape[0]].set(a.astype(F32))
    bands = jnp.linspace(1e-4, HY_BANDS - 1, HY_BANDS, dtype=F32)
    band = jnp.zeros((1, LANE), F32).at[0, 1:1 + HY_BANDS].set(bands).at[0, 1 + HY_BANDS:1 + 2 * HY_BANDS].set(bands)
    w4 = jnp.zeros((LANE, 2 * E), F32).at[:HY_HIDDEN].set(f_w4.astype(F32))
    max_decay = math.log(HY_DECAY_TARGET) / HY_SHORT_DECAY_PCT
    min_decay = math.log(HY_DECAY_TARGET) / HY_LONG_DECAY_PCT
    deltas = jnp.abs(jnp.linspace(min_decay, max_decay, E, dtype=F32)).reshape(1, E)
    sq = pl.BlockSpec((LANE, LANE), lambda i: (0, 0))
    vec = pl.BlockSpec((1, LANE), lambda i: (0, 0))
    return pl.pallas_call(
        functools.partial(_hy_filter_kernel, P=P, J=J, T=T),
        grid=(2 * J,),
        in_specs=[vec, sq, vec, sq, vec, sq, vec, vec,
                  pl.BlockSpec((LANE, E), lambda i: (0, jnp.where(i >= J, 0, 1))),
                  pl.BlockSpec((LANE, E), lambda i: (0, 1)),
                  pl.BlockSpec((1, E), lambda i: (0, 0))],
        out_specs=[pl.BlockSpec((P, E), lambda i: (i, 0)),
                   pl.BlockSpec((1, E), lambda i: (0, 0))],
        out_shape=[jax.ShapeDtypeStruct((2 * J * P, E), F32),
                   jax.ShapeDtypeStruct((1, E), F32)],
        compiler_params=_params("arbitrary"),
        name="hyena_filter",
    )(band, pad2(f_w1), padv(f_b1), pad2(f_w2), padv(f_b2), pad2(f_w3), padv(f_b3), padv(f_freq),
      w4, w4, deltas)


def _hy_spec_kernel(lo_ref, hi_ref, flo_ref, fhi_ref, n_ref, gr_ref, gi_ref, *, P):
    spec = (jnp.dot(flo_ref[...], lo_ref[...].astype(BF16), preferred_element_type=F32)
            + jnp.dot(fhi_ref[...], hi_ref[...].astype(BF16), preferred_element_type=F32))
    scale = 1.0 / ((n_ref[...] + HY_FILTER_EPS) * P)
    gr_ref[0] = spec[:P] * scale
    gi_ref[0] = spec[P:] * scale


def _hy_spectrum(g, norm, fwd, lo, P, J):
    E = g.shape[1]
    cs = _tile(E, 256, LANE)
    mat = pl.BlockSpec((2 * P, P), lambda c, q: (0, 0))
    out = pl.BlockSpec((1, P, cs), lambda c, q: (q, 0, c))
    shape = jax.ShapeDtypeStruct((2 * J - 1, P, E), F32)
    return pl.pallas_call(
        functools.partial(_hy_spec_kernel, P=P),
        grid=(E // cs, 2 * J - 1),
        in_specs=[pl.BlockSpec((P, cs), lambda c, q: (q, c)),
                  pl.BlockSpec((P, cs), lambda c, q: (q + 1, c)),
                  mat, mat,
                  pl.BlockSpec((1, cs), lambda c, q: (0, c))],
        out_specs=[out, out],
        out_shape=[shape, shape],
        compiler_params=_params("parallel", "parallel"),
        name="hyena_spectrum",
    )(g, g, lo, fwd, norm)


def _hy_prep_kernel(x0m, x0l, x0r, x1m, x1l, x1r, vm, vl, vr, z_ref, cw_ref, cb_ref,
                    w_ref, gg_ref, U_ref, *, tt, Tp, ce):
    row0 = pl.program_id(1) * tt
    valid = _row_valid(row0, tt, Tp)

    def conv(k, main, left, right):
        def masked(ref, base, n):
            return jnp.where(_row_valid(base, n, Tp), ref[0].astype(F32), 0.0)

        U_ref[0:HALO] = masked(left, row0 - HALO, HALO)
        U_ref[HALO:HALO + tt] = masked(main, row0, tt)
        U_ref[HALO + tt:HALO + tt + HALO] = masked(right, row0 + tt, HALO)
        cw = cw_ref[k]
        return (cw[0:1] * U_ref[HALO - 1:HALO - 1 + tt] + cw[1:2] * U_ref[HALO:HALO + tt]
                + cw[2:3] * U_ref[HALO + 1:HALO + 1 + tt] + cb_ref[k])

    x1 = conv(1, x1m, x1l, x1r)
    v = conv(2, vm, vl, vr)
    w_ref[0] = jnp.where(valid, v * x1, 0.0)
    x0 = conv(0, x0m, x0l, x0r)
    gg_ref[0] = jnp.where(valid, x0 * _silu(z_ref[0].astype(F32)), 0.0).astype(gg_ref.dtype)


def _hy_prep(u, conv_w, conv_b, E, TH):
    B, Tp, _ = u.shape
    ce = _tile(E, 512, LANE)
    nce = E // ce
    tt = _tile(math.gcd(Tp, TH), 384, HALO)
    specs = []
    for s in range(3):
        specs += _halo_specs(ce, tt, Tp, lambda c, s=s: s * nce + c)
    nt = Tp // tt
    cw = conv_w.astype(F32).reshape(HY_SHORT, 3, E).transpose(1, 0, 2)
    cb = conv_b.astype(F32).reshape(3, 1, E)
    return pl.pallas_call(
        functools.partial(_hy_prep_kernel, tt=tt, Tp=Tp, ce=ce),
        grid=(B, TH // tt, nce),
        in_specs=[*specs,
                  pl.BlockSpec((1, tt, ce), lambda b, i, c: (b, jnp.minimum(i, nt - 1), 3 * nce + c)),
                  pl.BlockSpec((3, HY_SHORT, ce), lambda b, i, c: (0, 0, c)),
                  pl.BlockSpec((3, 1, ce), lambda b, i, c: (0, 0, c))],
        out_specs=[pl.BlockSpec((1, tt, ce), lambda b, i, c: (b, i, c)),
                   pl.BlockSpec((1, tt, ce), lambda b, i, c: (b, i, c))],
        out_shape=[jax.ShapeDtypeStruct((B, TH, E), F32),
                   jax.ShapeDtypeStruct((B, TH, E), BF16)],
        scratch_shapes=[pltpu.VMEM((tt + 2 * HALO, ce), F32)],
        compiler_params=_params("parallel", "parallel", "parallel"),
        name="hyena_prep",
    )(*([u] * 10), cw, cb)


def _hy_conv_kernel(w_ref, gg_ref, fwd_ref, inv_ref, gr_ref, gi_ref, sk_ref, o_ref, yr_ref, yi_ref,
                    *, P, J):
    s = pl.program_id(2)
    nb = w_ref.shape[0]

    @pl.when(s == 0)
    def _():
        yr_ref[...] = jnp.zeros_like(yr_ref)
        yi_ref[...] = jnp.zeros_like(yi_ref)

    @pl.when(s < J)
    def _():
        wv = jnp.concatenate([w_ref[b] for b in range(nb)], axis=1).astype(BF16)
        spec = jnp.dot(fwd_ref[...], wv, preferred_element_type=F32)
        vr, vi = spec[:P], spec[P:]
        for i in range(J):
            q = i - s + J - 1
            gr = jnp.concatenate([gr_ref[q]] * nb, axis=1)
            gi = jnp.concatenate([gi_ref[q]] * nb, axis=1)
            yr_ref[i] += vr * gr - vi * gi
            yi_ref[i] += vr * gi + vi * gr

    @pl.when(s >= J)
    def _():
        i = s - J
        spec = jnp.concatenate([yr_ref[i], yi_ref[i]], axis=0).astype(BF16)
        y = jnp.dot(inv_ref[...], spec, preferred_element_type=F32)
        for b in range(nb):
            yb = y[:, b * LANE:(b + 1) * LANE] + sk_ref[...] * w_ref[b]
            o_ref[b] = (yb * gg_ref[b].astype(F32)).astype(o_ref.dtype)


def _hy_long_conv(w, gg, fwd, inv, gr, gi, skip, P, J):
    B, TH, E = w.shape
    nb = 2 if B % 2 == 0 else 1
    blk = lambda fn: pl.BlockSpec((nb, P, LANE), fn)
    spectra = pl.BlockSpec((2 * J - 1, P, LANE), lambda c, b, s: (0, 0, c))
    return pl.pallas_call(
        functools.partial(_hy_conv_kernel, P=P, J=J),
        grid=(E // LANE, B // nb, 2 * J),
        in_specs=[blk(lambda c, b, s: (b, jnp.where(s < J, s, s - J), c)),
                  blk(lambda c, b, s: (b, jnp.maximum(s - J, 0), c)),
                  pl.BlockSpec((2 * P, P), lambda c, b, s: (0, 0)),
                  pl.BlockSpec((P, 2 * P), lambda c, b, s: (0, 0)),
                  spectra, spectra,
                  pl.BlockSpec((1, LANE), lambda c, b, s: (0, c))],
        out_specs=blk(lambda c, b, s: (b, jnp.maximum(s - J, 0), c)),
        out_shape=jax.ShapeDtypeStruct((B, TH, E), BF16),
        scratch_shapes=[pltpu.VMEM((J, P, nb * LANE), F32), pltpu.VMEM((J, P, nb * LANE), F32)],
        compiler_params=_params("parallel", "parallel", "arbitrary"),
        name="hyena_long_conv",
    )(w, gg, fwd, inv, gr, gi, skip.reshape(1, E).astype(F32))


def _hyena_layer(h, gp, go, w_in, b_in, conv_w, conv_b, f_w1, f_b1, f_w2, f_b2, f_w3, f_b3, f_w4,
                 f_freq, skip, w_out):
    B, Tp, D = h.shape
    E = w_out.shape[0]
    T = Tp - FRONT
    P, J = _hy_block(Tp)
    u = _in_proj(h, gp, w_in.astype(BF16), b_in)
    g, norm = _hy_filter(f_w1, f_b1, f_w2, f_b2, f_w3, f_b3, f_w4, f_freq, E, T, P, J)
    fwd, lo, inv = _dft_mats(P)
    gr, gi = _hy_spectrum(g, norm, fwd, lo, P, J)
    w, gg = _hy_prep(u, conv_w, conv_b, E, J * P)
    a = _hy_long_conv(w, gg, fwd, inv, gr, gi, skip, P, J)
    return _out_proj(a, w_out.astype(BF16), jnp.zeros((D,), F32), go, h)


def _attention_layer(h, gp, go, w_in, lam_vecs, subln, w_out, layer_idx):
    B, Tp, D = h.shape
    E = w_out.shape[0]
    n_qk = (w_in.shape[1] - 2 * E) // 2
    u = _in_proj(h, gp, w_in.astype(BF16), jnp.zeros((w_in.shape[1],), F32))
    qk = _rope(u, n_qk, Tp)
    a = _diff_attention_core(u, qk, lam_vecs, subln, layer_idx, E)
    return _out_proj(a, w_out.astype(BF16), jnp.zeros((D,), F32), go, h)


def _gdn_layer(h, gp, go, w_in, conv_w, a_log, dt_bias, o_norm, w_out):
    B, Tp, D = h.shape
    E = w_out.shape[0]
    n_in = w_in.shape[1]
    n_conv = conv_w.shape[1]
    n_qk = (n_conv - E) // 2
    n_pad = -(-n_in // LANE) * LANE
    w_p = jnp.zeros((D, n_pad), BF16).at[:, :n_in].set(w_in.astype(BF16))
    u = _in_proj(h, gp, w_p, jnp.zeros((n_pad,), F32))
    q, k, kt, v, gb, gbt = _gdn_prep(u, conv_w, a_log, dt_bias, n_qk, E)
    o2 = _gdn_scan(q, k, kt, v, gb, gbt)
    return _gdn_out(o2, u, n_conv // E, o_norm, w_out.astype(BF16), go, h)


def _conformer_layer(h, gp, go, w_in, b_in, dw_w, dw_b, ln_g, ln_b, w_out, b_out):
    u = _in_proj(h, gp, w_in.astype(BF16), b_in)
    a = _conformer_core(u, dw_w, dw_b, ln_g, ln_b)
    return _out_proj(a, w_out.astype(BF16), b_out, go, h)


def kernel(x, meta, norm_pre, norm_post, hy_w_in, hy_b_in, hy_conv_w, hy_conv_b, hy_f_w1, hy_f_b1, hy_f_w2, hy_f_b2, hy_f_w3, hy_f_b3, hy_f_w4, hy_f_freq, hy_skip, hy_w_out, da_w_in, da_lambda, da_subln, da_w_out, gdn_w_in, gdn_conv_w, gdn_a_log, gdn_dt_bias, gdn_o_norm, gdn_w_out, cf_w_in, cf_b_in, cf_dw_w, cf_dw_b, cf_ln_g, cf_ln_b, cf_w_out, cf_b_out):
    B, S, D = x.shape
    assert S % LANE == 0
    h = jnp.concatenate([jnp.zeros((B, FRONT, D), F32),
                         jnp.broadcast_to(meta[None].astype(F32), (B, N_META, D)),
                         x.astype(F32)], axis=1)
    for i in range(norm_pre.shape[0]):
        m, j = i % 4, i // 4
        gp, go = norm_pre[i], norm_post[i]
        if m == 0:
            h = _hyena_layer(h, gp, go, hy_w_in[j], hy_b_in[j], hy_conv_w[j], hy_conv_b[j],
                             hy_f_w1[j], hy_f_b1[j], hy_f_w2[j], hy_f_b2[j], hy_f_w3[j], hy_f_b3[j],
                             hy_f_w4[j], hy_f_freq[j], hy_skip[j], hy_w_out[j])
        elif m == 1:
            h = _attention_layer(h, gp, go, da_w_in[j], da_lambda[j], da_subln[j], da_w_out[j], i)
        elif m == 2:
            h = _gdn_layer(h, gp, go, gdn_w_in[j], gdn_conv_w[j], gdn_a_log[j], gdn_dt_bias[j],
                           gdn_o_norm[j], gdn_w_out[j])
        else:
            h = _conformer_layer(h, gp, go, cf_w_in[j], cf_b_in[j], cf_dw_w[j], cf_dw_b[j],
                                 cf_ln_g[j], cf_ln_b[j], cf_w_out[j], cf_b_out[j])
    return h[:, FRONT + N_META:].astype(x.dtype)
```

```python
import functools
import math

import numpy as np
import jax
import jax.numpy as jnp
from jax import lax
from jax.experimental import pallas as pl
from jax.experimental.pallas import tpu as pltpu

F32 = jnp.float32
BF16 = jnp.bfloat16

N_META = 16
FRONT = 112
HALO = 16
LANE = 128
RMS_EPS = 1e-6
LN_EPS = 1e-5
ROPE_THETA = 10000.0
VMEM_LIMIT = 56 * 1024 * 1024

HY_SHORT = 3
HY_BANDS = 16
HY_HIDDEN = 64
HY_SHORT_DECAY_PCT = 0.3
HY_LONG_DECAY_PCT = 1.5
HY_DECAY_TARGET = 1e-2
HY_FILTER_EPS = 1e-6
HY_ROWS = 64
DA_HEAD_DIM = 64
DA_V_DIM = 128
DA_NORM_EPS = 1e-5
ATTN_ANCHOR_ROWS = 16
GDN_HEAD_DIM = 128
GDN_CHUNK = 128
CF_WIDTH = 31

_NT = (((1,), (1,)), ((), ()))


def _tile(n, target, mult=8):
    best = None
    for t in range(mult, min(n, target) + 1, mult):
        if n % t == 0:
            best = t
    assert best is not None, (n, target, mult)
    return best


def _params(*sem):
    return pltpu.CompilerParams(dimension_semantics=sem, vmem_limit_bytes=VMEM_LIMIT)


def _sigmoid(x):
    return 0.5 * jnp.tanh(0.5 * x) + 0.5


def _silu(x):
    h = 0.5 * x
    return h * jnp.tanh(h) + h


def _row_valid(base, n, Tp):
    rows = base + lax.broadcasted_iota(jnp.int32, (n, 1), 0)
    return (rows >= FRONT) & (rows < Tp)


def _in_proj_kernel(h_ref, g_ref, w_ref, b_ref, o_ref):
    x = h_ref[0]
    ms = jnp.mean(x * x, axis=-1, keepdims=True)
    y = (x * lax.rsqrt(ms + RMS_EPS) * g_ref[...]).astype(BF16)
    acc = jnp.dot(y, w_ref[...], preferred_element_type=F32)
    o_ref[0] = (acc + b_ref[...]).astype(o_ref.dtype)


def _in_proj(h, gain, w, bias):
    B, Tp, D = h.shape
    N = w.shape[1]
    tm = _tile(Tp, 1056)
    tn = _tile(N, 2048, LANE)
    return pl.pallas_call(
        _in_proj_kernel,
        grid=(N // tn, B, Tp // tm),
        in_specs=[
            pl.BlockSpec((1, tm, D), lambda n, b, i: (b, i, 0)),
            pl.BlockSpec((1, D), lambda n, b, i: (0, 0)),
            pl.BlockSpec((D, tn), lambda n, b, i: (0, n)),
            pl.BlockSpec((1, tn), lambda n, b, i: (0, n)),
        ],
        out_specs=pl.BlockSpec((1, tm, tn), lambda n, b, i: (b, i, n)),
        out_shape=jax.ShapeDtypeStruct((B, Tp, N), BF16),
        compiler_params=_params("parallel", "parallel", "parallel"),
        name="in_proj",
    )(h, gain.reshape(1, D).astype(F32), w, bias.reshape(1, N).astype(F32))


def _post_norm_residual(y, g_ref, h_ref, o_ref):
    ms = jnp.mean(y * y, axis=-1, keepdims=True)
    o_ref[0] = h_ref[0] + y * lax.rsqrt(ms + RMS_EPS) * g_ref[...]


def _out_proj_kernel(a_ref, w_ref, b_ref, g_ref, h_ref, o_ref):
    y = jnp.dot(a_ref[0], w_ref[...], preferred_element_type=F32) + b_ref[...]
    _post_norm_residual(y, g_ref, h_ref, o_ref)


def _out_proj(a, w, bias, gain, h):
    B, Tp, D = h.shape
    E = a.shape[2]
    tm = _tile(Tp, 528)
    return pl.pallas_call(
        _out_proj_kernel,
        grid=(B, Tp // tm),
        in_specs=[
            pl.BlockSpec((1, tm, E), lambda b, i: (b, i, 0)),
            pl.BlockSpec((E, D), lambda b, i: (0, 0)),
            pl.BlockSpec((1, D), lambda b, i: (0, 0)),
            pl.BlockSpec((1, D), lambda b, i: (0, 0)),
            pl.BlockSpec((1, tm, D), lambda b, i: (b, i, 0)),
        ],
        out_specs=pl.BlockSpec((1, tm, D), lambda b, i: (b, i, 0)),
        out_shape=jax.ShapeDtypeStruct((B, Tp, D), F32),
        compiler_params=_params("parallel", "parallel"),
        name="out_proj",
    )(a, w, bias.reshape(1, D).astype(F32), gain.reshape(1, D).astype(F32), h)


def _halo_specs(width, tt, Tp, col):
    nt, nh, r = Tp // tt, Tp // HALO, tt // HALO
    main = pl.BlockSpec((1, tt, width), lambda b, i, *_: (b, jnp.minimum(i, nt - 1), col(*_)))
    left = pl.BlockSpec((1, HALO, width),
                        lambda b, i, *_: (b, jnp.clip(i * r - 1, 0, nh - 1), col(*_)))
    right = pl.BlockSpec((1, HALO, width),
                         lambda b, i, *_: (b, jnp.clip((i + 1) * r, 0, nh - 1), col(*_)))
    return main, left, right


CF_ROWS = 64


def _cf_kernel(am, al, ar, gm, gl, gr, z_ref, w_ref, b_ref, lg_ref, lb_ref, o_ref, G_ref, Y_ref,
               *, tt, Tp, E):
    row0 = pl.program_id(1) * tt

    def glu(a_ref, g_ref, base, n):
        a = a_ref[0].astype(F32)
        g = g_ref[0].astype(F32)
        return jnp.where(_row_valid(base, n, Tp), a * _sigmoid(g), 0.0)

    G_ref[0:HALO] = glu(al, gl, row0 - HALO, HALO)
    G_ref[HALO:HALO + tt] = glu(am, gm, row0, tt)
    G_ref[HALO + tt:HALO + tt + HALO] = glu(ar, gr, row0 + tt, HALO)

    shift = HALO - CF_WIDTH // 2
    SUB = 8

    def lane_body(c, carry):
        off = pl.multiple_of(c * LANE, LANE)
        lanes = pl.ds(off, LANE)
        bias = b_ref[:, lanes]
        taps = [w_ref[k:k + 1, lanes] for k in range(CF_WIDTH)]
        for rb in range(tt // CF_ROWS):
            acc = None
            for res in range(SUB):
                part = None
                for k in range(CF_WIDTH):
                    if (k + shift) % SUB != res:
                        continue
                    base = rb * CF_ROWS + (k + shift) - res
                    term = taps[k] * G_ref[pl.ds(base, CF_ROWS + SUB), lanes]
                    part = term if part is None else part + term
                if part is None:
                    continue
                part = part[res:res + CF_ROWS]
                acc = part if acc is None else acc + part
            Y_ref[pl.ds(rb * CF_ROWS, CF_ROWS), lanes] = acc + bias
        return carry

    lax.fori_loop(0, E // LANE, lane_body, 0)

    y = Y_ref[...]
    mu = jnp.mean(y, axis=-1, keepdims=True)
    yc = y - mu
    var = jnp.mean(yc * yc, axis=-1, keepdims=True)
    yn = yc * lax.rsqrt(var + LN_EPS) * lg_ref[...] + lb_ref[...]
    o_ref[0] = (_silu(yn) * _silu(z_ref[0].astype(F32))).astype(o_ref.dtype)


def _conformer_core(u, dw_w, dw_b, ln_g, ln_b):
    B, Tp, E3 = u.shape
    E = E3 // 3
    tt = _tile(Tp, 384, CF_ROWS)
    a_specs = _halo_specs(E, tt, Tp, lambda: 0)
    g_specs = _halo_specs(E, tt, Tp, lambda: 1)
    vec = pl.BlockSpec((1, E), lambda b, i: (0, 0))
    return pl.pallas_call(
        functools.partial(_cf_kernel, tt=tt, Tp=Tp, E=E),
        grid=(B, Tp // tt),
        in_specs=[*a_specs, *g_specs,
                  pl.BlockSpec((1, tt, E), lambda b, i: (b, i, 2)),
                  pl.BlockSpec((CF_WIDTH, E), lambda b, i: (0, 0)),
                  vec, vec, vec],
        out_specs=pl.BlockSpec((1, tt, E), lambda b, i: (b, i, 0)),
        out_shape=jax.ShapeDtypeStruct((B, Tp, E), BF16),
        scratch_shapes=[pltpu.VMEM((tt + 2 * HALO, E), F32), pltpu.VMEM((tt, E), F32)],
        compiler_params=_params("parallel", "parallel"),
        name="conformer_conv",
    )(u, u, u, u, u, u, u, dw_w.astype(F32), dw_b.reshape(1, E).astype(F32),
      ln_g.reshape(1, E).astype(F32), ln_b.reshape(1, E).astype(F32))


def _rope_kernel(u_ref, c_ref, s_ref, o_ref, *, n_q, scale):
    cos = c_ref[...]
    sin = s_ref[...]
    lane = lax.broadcasted_iota(jnp.int32, (1, LANE), 1)
    first = (lane % DA_HEAD_DIM) < DA_HEAD_DIM // 2
    for g in range(u_ref.shape[2] // LANE):
        x = u_ref[0, :, g * LANE:(g + 1) * LANE].astype(F32)
        partner = jnp.where(first, pltpu.roll(x, LANE - DA_HEAD_DIM // 2, axis=1),
                            pltpu.roll(x, DA_HEAD_DIM // 2, axis=1))
        r = x * cos + partner * sin
        if g < n_q:
            r = r * scale
        o_ref[0, :, g * LANE:(g + 1) * LANE] = r.astype(o_ref.dtype)


def _rope(u, n_qk, Tp):
    B = u.shape[0]
    half = DA_HEAD_DIM // 2
    inv_freq = ROPE_THETA ** (-jnp.arange(0, DA_HEAD_DIM, 2, dtype=F32) / DA_HEAD_DIM)
    pos = (jnp.arange(Tp, dtype=jnp.int32) - FRONT).astype(F32)
    ang = pos[:, None] * inv_freq[None, :]
    cos = jnp.tile(jnp.cos(ang), (1, LANE // half))
    sin = jnp.tile(jnp.concatenate([-jnp.sin(ang), jnp.sin(ang)], axis=1), (1, LANE // DA_HEAD_DIM))
    tm = _tile(Tp, 384)
    return pl.pallas_call(
        functools.partial(_rope_kernel, n_q=n_qk // LANE, scale=DA_HEAD_DIM ** -0.5 * math.log2(math.e)),
        grid=(B, Tp // tm),
        in_specs=[pl.BlockSpec((1, tm, 2 * n_qk), lambda b, i: (b, i, 0)),
                  pl.BlockSpec((tm, LANE), lambda b, i: (i, 0)),
                  pl.BlockSpec((tm, LANE), lambda b, i: (i, 0))],
        out_specs=pl.BlockSpec((1, tm, 2 * n_qk), lambda b, i: (b, i, 0)),
        out_shape=jax.ShapeDtypeStruct((B, Tp, 2 * n_qk), BF16),
        compiler_params=_params("parallel", "parallel"),
        name="rope",
    )(u, cos, sin)


def _attn_kernel(lv_ref, sub_ref, az_ref, q_ref, k_ref, v_ref, z_ref, o_ref, vx_ref, s_ref,
                 *, lam_init, Tp, kc, n_sub):
    first = FRONT + N_META

    @pl.when(pl.program_id(2) == 0)
    def _():
        vx_ref[:, :LANE] = v_ref[0]
        vx_ref[:, LANE:] = jnp.ones((Tp, LANE), BF16)

    lane = lax.broadcasted_iota(jnp.int32, (1, LANE), 1)
    chunks = [(0, first)] + [(r0, kc) for r0 in range(first, Tp, kc)]
    kvalid = lax.broadcasted_iota(jnp.int32, (1, first), 1) >= FRONT
    nc = len(chunks)
    n_acc = 2
    lv = lv_ref[...]
    lam = (jnp.exp(jnp.sum(lv[0:1] * lv[1:2], axis=-1, keepdims=True))
           - jnp.exp(jnp.sum(lv[2:3] * lv[3:4], axis=-1, keepdims=True)) + lam_init)

    def row_max(first_chunk, part):
        return jnp.maximum(jnp.max(part, axis=-1, keepdims=True),
                           jnp.max(first_chunk, axis=-1, keepdims=True))

    def accumulate(accs, t, c):
        accs[c % n_acc] = t if accs[c % n_acc] is None else accs[c % n_acc] + t

    tq = q_ref.shape[1]
    ts = tq // n_sub
    anchor = {}
    qmask = {}
    stats = {}

    def masked_q(t):
        if t not in qmask:
            q = q_ref[0, t * ts:(t + 1) * ts, :]
            if t - 1 in anchor:
                q = q + jnp.tile(anchor[t - 1], (ts // ATTN_ANCHOR_ROWS, 1))
            zero = jnp.zeros_like(q)
            qmask[t] = [jnp.where(lane < DA_HEAD_DIM, q, zero), jnp.where(lane >= DA_HEAD_DIM, q, zero)]
        return qmask[t]

    def scores(t, m, c):
        r0, n = chunks[c]
        s = lax.dot_general(masked_q(t)[m], k_ref[0, r0:r0 + n, :], _NT, preferred_element_type=F32)
        st = stats.setdefault((t, m), [None, None])
        if c == 0:
            s = jnp.where(kvalid, s, -1e30)
            st[0] = s
        else:
            st[1] = s if st[1] is None else jnp.maximum(st[1], s)
        s_ref[2 * (t % 2) + m, :, r0:r0 + n] = s

    def weighted(t, m, mx, c):
        r0, n = chunks[c]
        p = jnp.exp2((s_ref[2 * (t % 2) + m, :, r0:r0 + n] - mx).astype(BF16))
        if m == 0 and c == 0:
            anchor[t] = p[0:ATTN_ANCHOR_ROWS, 0:LANE] * az_ref[...]
        return jnp.dot(p, vx_ref[r0:r0 + n, :], preferred_element_type=F32)

    for m in range(2):
        for c in range(nc):
            scores(0, m, c)
    for t in range(n_sub):
        rows = slice(t * ts, (t + 1) * ts)
        outs = []
        for m in range(2):
            mx = row_max(*stats[(t, m)])
            accs = [None] * n_acc
            for c in range(nc):
                accumulate(accs, weighted(t, m, mx, c), c)
                if t + 1 < n_sub:
                    scores(t + 1, m, c)
            acc = functools.reduce(lambda x, y: x + y, accs)
            outs.append(acc[:, :LANE] / acc[:, LANE:LANE + 1])
        o = outs[0] - lam * outs[1]
        ms = jnp.mean(o * o, axis=-1, keepdims=True)
        o = o * lax.rsqrt(ms + DA_NORM_EPS) * sub_ref[...] * (1.0 - lam_init)
        o_ref[0, rows, :] = (o * _silu(z_ref[0, rows, :].astype(F32))).astype(o_ref.dtype)


def _diff_attention_core(u, qk, lam_vecs, subln, layer_idx, E):
    B, Tp, _ = u.shape
    H = E // DA_V_DIM
    tq = _tile(Tp, 1056)
    kc = 2 * LANE
    assert (Tp - FRONT - N_META) % kc == 0
    lam_init = 0.8 - 0.6 * math.exp(-0.3 * layer_idx)
    n_sub = max(n for n in range(1, 9) if tq % (ATTN_ANCHOR_ROWS * n) == 0)
    anchor_zeros = jnp.zeros((ATTN_ANCHOR_ROWS, LANE), BF16)
    return pl.pallas_call(
        functools.partial(_attn_kernel, lam_init=lam_init, Tp=Tp, kc=kc, n_sub=n_sub),
        grid=(B, H, Tp // tq),
        in_specs=[
            pl.BlockSpec((4, DA_HEAD_DIM), lambda b, h, i: (0, 0)),
            pl.BlockSpec((1, DA_V_DIM), lambda b, h, i: (0, 0)),
            pl.BlockSpec((ATTN_ANCHOR_ROWS, LANE), lambda b, h, i: (0, 0)),
            pl.BlockSpec((1, tq, LANE), lambda b, h, i: (b, i, h)),
            pl.BlockSpec((1, Tp, LANE), lambda b, h, i: (b, 0, H + h)),
            pl.BlockSpec((1, Tp, LANE), lambda b, h, i: (b, 0, 2 * H + h)),
            pl.BlockSpec((1, tq, LANE), lambda b, h, i: (b, i, 3 * H + h)),
        ],
        out_specs=pl.BlockSpec((1, tq, LANE), lambda b, h, i: (b, i, h)),
        out_shape=jax.ShapeDtypeStruct((B, Tp, E), BF16),
        scratch_shapes=[pltpu.VMEM((Tp, 2 * LANE), BF16), pltpu.VMEM((4, tq // n_sub, Tp), F32)],
        compiler_params=_params("parallel", "parallel", "arbitrary"),
        name="diff_attention",
    )(lam_vecs.astype(F32), subln.reshape(1, DA_V_DIM).astype(F32), anchor_zeros, qk, qk, u, u)


def _gdn_prep_kernel(um, ul, ur, ab_ref, cw_ref, al_ref, dt_ref,
                     q_ref, k_ref, kt_ref, v_ref, gb_ref, gbt_ref, U_ref, *, tt, Tp, n_qk, n_g):
    row0 = pl.program_id(1) * tt

    def masked(ref, base, n):
        return jnp.where(_row_valid(base, n, Tp), ref[0].astype(F32), 0.0)

    U_ref[0:HALO] = masked(ul, row0 - HALO, HALO)
    U_ref[HALO:HALO + tt] = masked(um, row0, tt)
    U_ref[HALO + tt:HALO + tt + HALO] = masked(ur, row0 + tt, HALO)
    valid = _row_valid(row0, tt, Tp)

    def conv_silu(c0, width):
        cols = slice(c0, c0 + width)
        y = (cw_ref[0:1, cols] * U_ref[HALO - 1:HALO - 1 + tt, cols]
             + cw_ref[1:2, cols] * U_ref[HALO:HALO + tt, cols]
             + cw_ref[2:3, cols] * U_ref[HALO + 1:HALO + 1 + tt, cols])
        return _silu(y)

    def l2n(x):
        return x * lax.rsqrt(jnp.sum(x * x, axis=-1, keepdims=True) + 1e-6)

    for hh in range(n_qk // LANE):
        c0 = hh * LANE
        qh = l2n(conv_silu(c0, LANE)) * GDN_HEAD_DIM ** -0.5
        q_ref[0, :, c0:c0 + LANE] = jnp.where(valid, qh, 0.0).astype(q_ref.dtype)
        kh = jnp.where(valid, l2n(conv_silu(n_qk + c0, LANE)), 0.0)
        k_ref[0, :, c0:c0 + LANE] = kh.astype(k_ref.dtype)
        kt_ref[0, c0:c0 + LANE, :] = kh.T.astype(kt_ref.dtype)
    n_v = v_ref.shape[2]
    for c0 in range(0, n_v, LANE):
        vh = conv_silu(2 * n_qk + c0, LANE)
        v_ref[0, :, c0:c0 + LANE] = jnp.where(valid, vh, 0.0).astype(v_ref.dtype)

    x = ab_ref[0].astype(F32)
    lane = lax.broadcasted_iota(jnp.int32, (1, LANE), 1)
    xa = x + dt_ref[...]
    softplus = jnp.maximum(xa, 0.0) + jnp.log(1.0 + jnp.exp(-jnp.abs(xa)))
    g = -jnp.exp(al_ref[...]) * softplus
    beta = _sigmoid(x)
    gb = jnp.where(lane < n_g, g, jnp.where(lane < 2 * n_g, beta, 0.0))
    gb = jnp.where(valid, gb, 0.0)

    H = n_g // 2
    C = GDN_CHUNK
    r = lax.broadcasted_iota(jnp.int32, (C, C), 0)
    c = lax.broadcasted_iota(jnp.int32, (C, C), 1)
    lower = (c <= r).astype(F32)
    upper = (c >= r).astype(F32)
    hi = lax.Precision.HIGHEST

    def pack(gc, bt, tot):
        return jnp.where(lane < H, gc, jnp.where(lane < 2 * H, bt, jnp.where(lane < 3 * H, tot, 0.0)))

    for c3 in range(tt // C):
        rows = slice(c3 * C, (c3 + 1) * C)
        gch = gb[rows]
        cf = jnp.dot(lower, gch, precision=hi, preferred_element_type=F32)
        cb = jnp.dot(upper, gch, precision=hi, preferred_element_type=F32)
        tot = jnp.broadcast_to(jnp.sum(gch, axis=0, keepdims=True), (C, LANE))
        g0 = pack(cf, pltpu.roll(gch, LANE - H, axis=1), pltpu.roll(tot, 2 * H, axis=1))
        g1 = pack(pltpu.roll(cb, LANE - H, axis=1), pltpu.roll(gch, LANE - 2 * H, axis=1),
                  pltpu.roll(tot, H, axis=1))
        gb_ref[0, 0, rows, :] = g0
        gb_ref[1, 0, rows, :] = g1
        gbt_ref[0, 0, :, rows] = g0.T
        gbt_ref[1, 0, :, rows] = g1.T


def _gdn_prep(u, conv_w, a_log, dt_bias, n_qk, E):
    B, Tp, _ = u.shape
    n_conv = 2 * n_qk + E
    n_g = a_log.size
    assert 3 * (n_g // 2) <= LANE
    tt = _tile(Tp, 384, GDN_CHUNK)
    specs = _halo_specs(n_conv, tt, Tp, lambda: 0)
    al = jnp.zeros((1, LANE), F32).at[0, :n_g].set(a_log.reshape(-1).astype(F32))
    dt = jnp.zeros((1, LANE), F32).at[0, :n_g].set(dt_bias.reshape(-1).astype(F32))
    vec = pl.BlockSpec((1, LANE), lambda b, i: (0, 0))
    return pl.pallas_call(
        functools.partial(_gdn_prep_kernel, tt=tt, Tp=Tp, n_qk=n_qk, n_g=n_g),
        grid=(B, Tp // tt),
        in_specs=[*specs,
                  pl.BlockSpec((1, tt, LANE), lambda b, i: (b, i, (n_conv + E) // LANE)),
                  pl.BlockSpec((HY_SHORT, n_conv), lambda b, i: (0, 0)),
                  vec, vec],
        out_specs=[
            pl.BlockSpec((1, tt, n_qk), lambda b, i: (b, i, 0)),
            pl.BlockSpec((1, tt, n_qk), lambda b, i: (b, i, 0)),
            pl.BlockSpec((1, n_qk, tt), lambda b, i: (b, 0, i)),
            pl.BlockSpec((1, tt, E), lambda b, i: (b, i, 0)),
            pl.BlockSpec((2, 1, tt, LANE), lambda b, i: (0, b, i, 0)),
            pl.BlockSpec((2, 1, LANE, tt), lambda b, i: (0, b, 0, i)),
        ],
        out_shape=[
            jax.ShapeDtypeStruct((B, Tp, n_qk), BF16),
            jax.ShapeDtypeStruct((B, Tp, n_qk), BF16),
            jax.ShapeDtypeStruct((B, n_qk, Tp), BF16),
            jax.ShapeDtypeStruct((B, Tp, E), BF16),
            jax.ShapeDtypeStruct((2, B, Tp, LANE), F32),
            jax.ShapeDtypeStruct((2, B, LANE, Tp), F32),
        ],
        scratch_shapes=[pltpu.VMEM((tt + 2 * HALO, n_conv), F32)],
        compiler_params=_params("parallel", "parallel"),
        name="gdn_prep",
    )(u, u, u, u, conv_w.astype(F32), al, dt)


def _mm(a, b):
    return jnp.dot(a.astype(BF16), b.astype(BF16), preferred_element_type=F32)


def _unit_lower_inverse(mats):
    C = mats[0].shape[0]
    row = lax.broadcasted_iota(jnp.int32, (C, C), 0)
    col = lax.broadcasted_iota(jnp.int32, (C, C), 1)
    rc = row ^ col
    ps = [jnp.where(rc < 8, a, 0.0) for a in mats]
    xs = [jnp.where(rc == 0, 1.0, 0.0) - p for p in ps]
    n = 2
    while n < 8:
        ps = [_mm(p, p) for p in ps]
        xs = [x + _mm(x, p) for x, p in zip(xs, ps)]
        n *= 2
    b = 8
    while b < C:
        sel = (rc >= b) & (rc < 2 * b)
        ys = [_mm(x, jnp.where(sel, a, 0.0)) for x, a in zip(xs, mats)]
        xs = [x - _mm(y, x) for x, y in zip(xs, ys)]
        b *= 2
    return xs


def _gdn_chunk_kernel(q_ref, k_ref, kt_ref, v_ref, gb_ref, gbt_ref, o_ref, S_ref, *, n_heads):
    d = pl.program_id(0)

    @pl.when(pl.program_id(2) == 0)
    def _():
        S_ref[...] = jnp.zeros_like(S_ref)

    C = q_ref.shape[1]
    H = n_heads
    rep = n_heads // (q_ref.shape[2] // GDN_HEAD_DIM)
    row = lax.broadcasted_iota(jnp.int32, (C, C), 0)
    col = lax.broadcasted_iota(jnp.int32, (C, C), 1)
    ahead = (row - col) * (1 - 2 * d)
    incl = ahead >= 0
    strict = ahead > 0
    gb = gb_ref[0, 0]
    gbt = gbt_ref[0, 0]

    heads = range(n_heads)
    kcols = [slice((hv // rep) * LANE, (hv // rep + 1) * LANE) for hv in heads]
    kk, qk = [], []
    for kh in range(n_heads // rep):
        cols = slice(kh * LANE, (kh + 1) * LANE)
        k = k_ref[0, :, cols]
        kk.append(lax.dot_general(k, k, _NT, preferred_element_type=F32))
        qk.append(lax.dot_general(q_ref[0, :, cols], k, _NT, preferred_element_type=F32))

    gc = [gb[:, hv:hv + 1] for hv in heads]
    beta = [gb[:, H + hv:H + hv + 1] for hv in heads]
    gc_row = [gbt[hv:hv + 1, :] for hv in heads]
    gtot_row = [gbt[2 * H + hv:2 * H + hv + 1, :] for hv in heads]
    decay = [jnp.exp(jnp.where(incl, gc[hv] - gc_row[hv], -1e30)) for hv in heads]
    a = [jnp.where(strict, beta[hv] * kk[hv // rep] * decay[hv], 0.0) for hv in heads]
    t_inv = _unit_lower_inverse(a)
    e_gc = [jnp.exp(g) for g in gc]
    rhs = [jnp.concatenate([v_ref[0, :, hv * LANE:(hv + 1) * LANE].astype(F32) * beta[hv],
                            k_ref[0, :, kcols[hv]].astype(F32) * (beta[hv] * e_gc[hv])], axis=1)
           for hv in heads]
    sol = [_mm(t_inv[hv], rhs[hv]) for hv in heads]
    s_old = [S_ref[hv] for hv in heads]
    r = [_mm(jnp.concatenate([sol[hv][:, LANE:], q_ref[0, :, kcols[hv]].astype(F32) * e_gc[hv]], axis=0),
             s_old[hv]) for hv in heads]
    v_new = [(sol[hv][:, :LANE] - r[hv][:C]).astype(BF16) for hv in heads]
    for hv in heads:
        o = r[hv][C:] + _mm(qk[hv // rep] * decay[hv], v_new[hv])
        o_ref[0, 0, :, hv * LANE:(hv + 1) * LANE] = o.astype(o_ref.dtype)
    for hv in heads:
        k_end_t = kt_ref[0, kcols[hv], :].astype(F32) * jnp.exp(gtot_row[hv] - gc_row[hv])
        S_ref[hv] = s_old[hv] * jnp.exp(gtot_row[hv][:, 0:1]) + _mm(k_end_t, v_new[hv])


def _gdn_scan(q, k, kt, v, gb, gbt):
    B, Tp, n_qk = q.shape
    E = v.shape[2]
    n_heads = E // GDN_HEAD_DIM
    C = GDN_CHUNK
    nC = Tp // C

    def cc(d, c):
        return c + d * (nC - 1 - 2 * c)

    return pl.pallas_call(
        functools.partial(_gdn_chunk_kernel, n_heads=n_heads),
        grid=(2, B, nC),
        in_specs=[
            pl.BlockSpec((1, C, n_qk), lambda d, b, c: (b, cc(d, c), 0)),
            pl.BlockSpec((1, C, n_qk), lambda d, b, c: (b, cc(d, c), 0)),
            pl.BlockSpec((1, n_qk, C), lambda d, b, c: (b, 0, cc(d, c))),
            pl.BlockSpec((1, C, E), lambda d, b, c: (b, cc(d, c), 0)),
            pl.BlockSpec((1, 1, C, LANE), lambda d, b, c: (d, b, cc(d, c), 0)),
            pl.BlockSpec((1, 1, LANE, C), lambda d, b, c: (d, b, 0, cc(d, c))),
        ],
        out_specs=pl.BlockSpec((1, 1, C, E), lambda d, b, c: (d, b, cc(d, c), 0)),
        out_shape=jax.ShapeDtypeStruct((2, B, Tp, E), BF16),
        scratch_shapes=[pltpu.VMEM((n_heads, GDN_HEAD_DIM, GDN_HEAD_DIM), F32)],
        compiler_params=_params("parallel", "parallel", "arbitrary"),
        name="gdn_scan",
    )(q, k, kt, v, gb, gbt)


def _gdn_out_kernel(of_ref, ob_ref, z_ref, on_ref, w_ref, g_ref, h_ref, o_ref):
    E = of_ref.shape[3]
    parts = []
    for c0 in range(0, E, LANE):
        o = of_ref[0, 0, :, c0:c0 + LANE].astype(F32) + ob_ref[0, 0, :, c0:c0 + LANE].astype(F32)
        ms = jnp.mean(o * o, axis=-1, keepdims=True)
        o = o * lax.rsqrt(ms + RMS_EPS) * on_ref[...]
        parts.append((o * _silu(z_ref[0, :, c0:c0 + LANE].astype(F32))).astype(BF16))
    a = jnp.concatenate(parts, axis=1)
    y = jnp.dot(a, w_ref[...], preferred_element_type=F32)
    _post_norm_residual(y, g_ref, h_ref, o_ref)


def _gdn_out(o2, u, z_col, o_norm, w, gain, h):
    B, Tp, D = h.shape
    E = o2.shape[3]
    tm = _tile(Tp, 528)
    return pl.pallas_call(
        _gdn_out_kernel,
        grid=(B, Tp // tm),
        in_specs=[
            pl.BlockSpec((1, 1, tm, E), lambda b, i: (0, b, i, 0)),
            pl.BlockSpec((1, 1, tm, E), lambda b, i: (1, b, i, 0)),
            pl.BlockSpec((1, tm, E), lambda b, i: (b, i, z_col)),
            pl.BlockSpec((1, GDN_HEAD_DIM), lambda b, i: (0, 0)),
            pl.BlockSpec((E, D), lambda b, i: (0, 0)),
            pl.BlockSpec((1, D), lambda b, i: (0, 0)),
            pl.BlockSpec((1, tm, D), lambda b, i: (b, i, 0)),
        ],
        out_specs=pl.BlockSpec((1, tm, D), lambda b, i: (b, i, 0)),
        out_shape=jax.ShapeDtypeStruct((B, Tp, D), F32),
        compiler_params=_params("parallel", "parallel"),
        name="gdn_out",
    )(o2, o2, u, o_norm.reshape(1, GDN_HEAD_DIM).astype(F32), w, gain.reshape(1, D).astype(F32), h)


def _hy_block(Tp):
    P = 768 if Tp >= 1536 else 128
    return P, -(-Tp // P)


def _dft_mats(P):
    k = np.arange(P, dtype=np.int64)[:, None]
    b = np.arange(P, dtype=np.int64)[None, :]

    def mat(m):
        ang = np.pi * (((2 * k + 1) * m) % (4 * P)).astype(np.float64) / (2 * P)
        return np.concatenate([np.cos(ang), -np.sin(ang)], axis=0).astype(np.float32)

    fwd = mat(b)
    lo = mat(b - P)
    as_bf16 = lambda a: jnp.asarray(a).astype(BF16)
    return as_bf16(fwd), as_bf16(lo), as_bf16(np.ascontiguousarray(fwd.T))


def _hy_filter_kernel(band_ref, w1_ref, b1_ref, w2_ref, b2_ref, w3_ref, b3_ref, fr_ref,
                      w4_ref, w4b_ref, dl_ref, g_ref, n_ref, *, P, J, T):
    i = pl.program_id(0)
    hi = lax.Precision.HIGHEST
    d = i * P + lax.broadcasted_iota(jnp.int32, (P, 1), 0) - J * P
    s = jnp.abs(d).astype(F32)
    t = s / (T - 1)
    w = (2.0 * math.pi) * s / T
    lane = lax.broadcasted_iota(jnp.int32, (1, LANE), 1)
    arg = band_ref[...] * w
    z = jnp.where(lane == 0, t,
                  jnp.where(lane <= HY_BANDS, jnp.cos(arg),
                            jnp.where(lane <= 2 * HY_BANDS, -jnp.sin(arg), 0.0)))
    fr = fr_ref[...]
    hdn = jnp.sin(fr * (jnp.dot(z, w1_ref[...], precision=hi, preferred_element_type=F32) + b1_ref[...]))
    hdn = jnp.sin(fr * (jnp.dot(hdn, w2_ref[...], precision=hi, preferred_element_type=F32) + b2_ref[...]))
    hdn = jnp.sin(fr * (jnp.dot(hdn, w3_ref[...], precision=hi, preferred_element_type=F32) + b3_ref[...]))
    filt = jnp.dot(hdn, w4_ref[...], precision=hi, preferred_element_type=F32) * jnp.exp(-t * dl_ref[...])
    g = jnp.where(s < T, filt, 0.0)
    g_ref[...] = g
    part = jnp.sum(jnp.abs(g), axis=0, keepdims=True)

    @pl.when(i == 0)
    def _():
        n_ref[...] = jnp.zeros_like(n_ref)

    n_ref[...] += part

    @pl.when(i == J)
    def _():
        extra = jnp.dot(hdn[0:8], w4b_ref[...], precision=hi, preferred_element_type=F32)
        n_ref[...] += jnp.abs(extra[0:1])


def _hy_filter(f_w1, f_b1, f_w2, f_b2, f_w3, f_b3, f_w4, f_freq, E, T, P, J):
    pad2 = lambda a: jnp.zeros((LANE, LANE), F32).at[:a.shape[0], :a.shape[1]].set(a.astype(F32))
    padv = lambda a: jnp.zeros((1, LANE), F32).at[0, :a.shape[0]].set(a.astype(F32))
    bands = jnp.linspace(1e-4, HY_BANDS - 1, HY_BANDS, dtype=F32)
    band = jnp.zeros((1, LANE), F32).at[0, 1:1 + HY_BANDS].set(bands).at[0, 1 + HY_BANDS:1 + 2 * HY_BANDS].set(bands)
    w4 = jnp.zeros((LANE, 2 * E), F32).at[:HY_HIDDEN].set(f_w4.astype(F32))
    max_decay = math.log(HY_DECAY_TARGET) / HY_SHORT_DECAY_PCT
    min_decay = math.log(HY_DECAY_TARGET) / HY_LONG_DECAY_PCT
    deltas = jnp.abs(jnp.linspace(min_decay, max_decay, E, dtype=F32)).reshape(1, E)
    sq = pl.BlockSpec((LANE, LANE), lambda i: (0, 0))
    vec = pl.BlockSpec((1, LANE), lambda i: (0, 0))
    return pl.pallas_call(
        functools.partial(_hy_filter_kernel, P=P, J=J, T=T),
        grid=(2 * J,),
        in_specs=[vec, sq, vec, sq, vec, sq, vec, vec,
                  pl.BlockSpec((LANE, E), lambda i: (0, jnp.where(i >= J, 0, 1))),
                  pl.BlockSpec((LANE, E), lambda i: (0, 1)),
                  pl.BlockSpec((1, E), lambda i: (0, 0))],
        out_specs=[pl.BlockSpec((P, E), lambda i: (i, 0)),
                   pl.BlockSpec((1, E), lambda i: (0, 0))],
        out_shape=[jax.ShapeDtypeStruct((2 * J * P, E), F32),
                   jax.ShapeDtypeStruct((1, E), F32)],
        compiler_params=_params("arbitrary"),
        name="hyena_filter",
    )(band, pad2(f_w1), padv(f_b1), pad2(f_w2), padv(f_b2), pad2(f_w3), padv(f_b3), padv(f_freq),
      w4, w4, deltas)


def _hy_spec_kernel(lo_ref, hi_ref, flo_ref, fhi_ref, n_ref, gr_ref, gi_ref, *, P):
    spec = (jnp.dot(flo_ref[...], lo_ref[...].astype(BF16), preferred_element_type=F32)
            + jnp.dot(fhi_ref[...], hi_ref[...].astype(BF16), preferred_element_type=F32))
    scale = 1.0 / ((n_ref[...] + HY_FILTER_EPS) * P)
    gr_ref[0] = spec[:P] * scale
    gi_ref[0] = spec[P:] * scale


def _hy_spectrum(g, norm, fwd, lo, P, J):
    E = g.shape[1]
    cs = _tile(E, 256, LANE)
    mat = pl.BlockSpec((2 * P, P), lambda c, q: (0, 0))
    out = pl.BlockSpec((1, P, cs), lambda c, q: (q, 0, c))
    shape = jax.ShapeDtypeStruct((2 * J - 1, P, E), F32)
    return pl.pallas_call(
        functools.partial(_hy_spec_kernel, P=P),
        grid=(E // cs, 2 * J - 1),
        in_specs=[pl.BlockSpec((P, cs), lambda c, q: (q, c)),
                  pl.BlockSpec((P, cs), lambda c, q: (q + 1, c)),
                  mat, mat,
                  pl.BlockSpec((1, cs), lambda c, q: (0, c))],
        out_specs=[out, out],
        out_shape=[shape, shape],
        compiler_params=_params("parallel", "parallel"),
        name="hyena_spectrum",
    )(g, g, lo, fwd, norm)


def _hy_prep_kernel(x0m, x0l, x0r, x1m, x1l, x1r, vm, vl, vr, z_ref, cw_ref, cb_ref,
                    w_ref, gg_ref, U_ref, *, tt, Tp, ce):
    row0 = pl.program_id(1) * tt
    valid = _row_valid(row0, tt, Tp)

    def conv(k, main, left, right):
        def masked(ref, base, n):
            return jnp.where(_row_valid(base, n, Tp), ref[0].astype(F32), 0.0)

        U_ref[0:HALO] = masked(left, row0 - HALO, HALO)
        U_ref[HALO:HALO + tt] = masked(main, row0, tt)
        U_ref[HALO + tt:HALO + tt + HALO] = masked(right, row0 + tt, HALO)
        cw = cw_ref[k]
        return (cw[0:1] * U_ref[HALO - 1:HALO - 1 + tt] + cw[1:2] * U_ref[HALO:HALO + tt]
                + cw[2:3] * U_ref[HALO + 1:HALO + 1 + tt] + cb_ref[k])

    x1 = conv(1, x1m, x1l, x1r)
    v = conv(2, vm, vl, vr)
    w_ref[0] = jnp.where(valid, v * x1, 0.0)
    x0 = conv(0, x0m, x0l, x0r)
    gg_ref[0] = jnp.where(valid, x0 * _silu(z_ref[0].astype(F32)), 0.0).astype(gg_ref.dtype)


def _hy_prep(u, conv_w, conv_b, E, TH):
    B, Tp, _ = u.shape
    ce = _tile(E, 512, LANE)
    nce = E // ce
    tt = _tile(math.gcd(Tp, TH), 384, HALO)
    specs = []
    for s in range(3):
        specs += _halo_specs(ce, tt, Tp, lambda c, s=s: s * nce + c)
    nt = Tp // tt
    cw = conv_w.astype(F32).reshape(HY_SHORT, 3, E).transpose(1, 0, 2)
    cb = conv_b.astype(F32).reshape(3, 1, E)
    return pl.pallas_call(
        functools.partial(_hy_prep_kernel, tt=tt, Tp=Tp, ce=ce),
        grid=(B, TH // tt, nce),
        in_specs=[*specs,
                  pl.BlockSpec((1, tt, ce), lambda b, i, c: (b, jnp.minimum(i, nt - 1), 3 * nce + c)),
                  pl.BlockSpec((3, HY_SHORT, ce), lambda b, i, c: (0, 0, c)),
                  pl.BlockSpec((3, 1, ce), lambda b, i, c: (0, 0, c))],
        out_specs=[pl.BlockSpec((1, tt, ce), lambda b, i, c: (b, i, c)),
                   pl.BlockSpec((1, tt, ce), lambda b, i, c: (b, i, c))],
        out_shape=[jax.ShapeDtypeStruct((B, TH, E), F32),
                   jax.ShapeDtypeStruct((B, TH, E), BF16)],
        scratch_shapes=[pltpu.VMEM((tt + 2 * HALO, ce), F32)],
        compiler_params=_params("parallel", "parallel", "parallel"),
        name="hyena_prep",
    )(*([u] * 10), cw, cb)


def _hy_conv_kernel(w_ref, gg_ref, fwd_ref, inv_ref, gr_ref, gi_ref, sk_ref, o_ref, yr_ref, yi_ref,
                    *, P, J):
    s = pl.program_id(2)
    nb = w_ref.shape[0]

    def forward(first_block):
        wv = jnp.concatenate([w_ref[b] for b in range(nb)], axis=1).astype(BF16)
        spec = jnp.dot(fwd_ref[...], wv, preferred_element_type=F32)
        for r0 in range(0, P, HY_ROWS):
            rows = slice(r0, r0 + HY_ROWS)
            vr, vi = spec[rows], spec[P + r0:P + r0 + HY_ROWS]
            for i in range(J):
                q = i - s + J - 1
                gr = jnp.concatenate([gr_ref[q, rows, :]] * nb, axis=1)
                gi = jnp.concatenate([gi_ref[q, rows, :]] * nb, axis=1)
                yr, yi = vr * gr - vi * gi, vr * gi + vi * gr
                if first_block:
                    yr_ref[i, rows, :] = yr
                    yi_ref[i, rows, :] = yi
                else:
                    yr_ref[i, rows, :] += yr
                    yi_ref[i, rows, :] += yi

    @pl.when(s == 0)
    def _():
        forward(True)

    @pl.when((s > 0) & (s < J))
    def _():
        forward(False)

    @pl.when(s >= J)
    def _():
        i = s - J
        spec = jnp.concatenate([yr_ref[i], yi_ref[i]], axis=0).astype(BF16)
        y = jnp.dot(inv_ref[...], spec, preferred_element_type=F32)
        for b in range(nb):
            yb = y[:, b * LANE:(b + 1) * LANE] + sk_ref[...] * w_ref[b]
            o_ref[b] = (yb * gg_ref[b].astype(F32)).astype(o_ref.dtype)


def _hy_long_conv(w, gg, fwd, inv, gr, gi, skip, P, J):
    B, TH, E = w.shape
    nb = 2 if B % 2 == 0 else 1
    blk = lambda fn: pl.BlockSpec((nb, P, LANE), fn)
    spectra = pl.BlockSpec((2 * J - 1, P, LANE), lambda c, b, s: (0, 0, c))
    return pl.pallas_call(
        functools.partial(_hy_conv_kernel, P=P, J=J),
        grid=(E // LANE, B // nb, 2 * J),
        in_specs=[blk(lambda c, b, s: (b, jnp.where(s < J, s, s - J), c)),
                  blk(lambda c, b, s: (b, jnp.maximum(s - J, 0), c)),
                  pl.BlockSpec((2 * P, P), lambda c, b, s: (0, 0)),
                  pl.BlockSpec((P, 2 * P), lambda c, b, s: (0, 0)),
                  spectra, spectra,
                  pl.BlockSpec((1, LANE), lambda c, b, s: (0, c))],
        out_specs=blk(lambda c, b, s: (b, jnp.maximum(s - J, 0), c)),
        out_shape=jax.ShapeDtypeStruct((B, TH, E), BF16),
        scratch_shapes=[pltpu.VMEM((J, P, nb * LANE), F32), pltpu.VMEM((J, P, nb * LANE), F32)],
        compiler_params=_params("parallel", "parallel", "arbitrary"),
        name="hyena_long_conv",
    )(w, gg, fwd, inv, gr, gi, skip.reshape(1, E).astype(F32))


def _hyena_layer(h, gp, go, w_in, b_in, conv_w, conv_b, f_w1, f_b1, f_w2, f_b2, f_w3, f_b3, f_w4,
                 f_freq, skip, w_out):
    B, Tp, D = h.shape
    E = w_out.shape[0]
    T = Tp - FRONT
    P, J = _hy_block(Tp)
    u = _in_proj(h, gp, w_in.astype(BF16), b_in)
    g, norm = _hy_filter(f_w1, f_b1, f_w2, f_b2, f_w3, f_b3, f_w4, f_freq, E, T, P, J)
    fwd, lo, inv = _dft_mats(P)
    gr, gi = _hy_spectrum(g, norm, fwd, lo, P, J)
    w, gg = _hy_prep(u, conv_w, conv_b, E, J * P)
    a = _hy_long_conv(w, gg, fwd, inv, gr, gi, skip, P, J)
    return _out_proj(a, w_out.astype(BF16), jnp.zeros((D,), F32), go, h)


def _attention_layer(h, gp, go, w_in, lam_vecs, subln, w_out, layer_idx):
    B, Tp, D = h.shape
    E = w_out.shape[0]
    n_qk = (w_in.shape[1] - 2 * E) // 2
    u = _in_proj(h, gp, w_in.astype(BF16), jnp.zeros((w_in.shape[1],), F32))
    qk = _rope(u, n_qk, Tp)
    a = _diff_attention_core(u, qk, lam_vecs, subln, layer_idx, E)
    return _out_proj(a, w_out.astype(BF16), jnp.zeros((D,), F32), go, h)


def _gdn_layer(h, gp, go, w_in, conv_w, a_log, dt_bias, o_norm, w_out):
    B, Tp, D = h.shape
    E = w_out.shape[0]
    n_in = w_in.shape[1]
    n_conv = conv_w.shape[1]
    n_qk = (n_conv - E) // 2
    n_pad = -(-n_in // LANE) * LANE
    w_p = jnp.zeros((D, n_pad), BF16).at[:, :n_in].set(w_in.astype(BF16))
    u = _in_proj(h, gp, w_p, jnp.zeros((n_pad,), F32))
    q, k, kt, v, gb, gbt = _gdn_prep(u, conv_w, a_log, dt_bias, n_qk, E)
    o2 = _gdn_scan(q, k, kt, v, gb, gbt)
    return _gdn_out(o2, u, n_conv // E, o_norm, w_out.astype(BF16), go, h)


def _conformer_layer(h, gp, go, w_in, b_in, dw_w, dw_b, ln_g, ln_b, w_out, b_out):
    u = _in_proj(h, gp, w_in.astype(BF16), b_in)
    a = _conformer_core(u, dw_w, dw_b, ln_g, ln_b)
    return _out_proj(a, w_out.astype(BF16), b_out, go, h)


def kernel(x, meta, norm_pre, norm_post, hy_w_in, hy_b_in, hy_conv_w, hy_conv_b, hy_f_w1, hy_f_b1, hy_f_w2, hy_f_b2, hy_f_w3, hy_f_b3, hy_f_w4, hy_f_freq, hy_skip, hy_w_out, da_w_in, da_lambda, da_subln, da_w_out, gdn_w_in, gdn_conv_w, gdn_a_log, gdn_dt_bias, gdn_o_norm, gdn_w_out, cf_w_in, cf_b_in, cf_dw_w, cf_dw_b, cf_ln_g, cf_ln_b, cf_w_out, cf_b_out):
    B, S, D = x.shape
    assert S % LANE == 0
    h = jnp.concatenate([jnp.zeros((B, FRONT, D), F32),
                         jnp.broadcast_to(meta[None].astype(F32), (B, N_META, D)),
                         x.astype(F32)], axis=1)
    for i in range(norm_pre.shape[0]):
        m, j = i % 4, i // 4
        gp, go = norm_pre[i], norm_post[i]
        if m == 0:
            h = _hyena_layer(h, gp, go, hy_w_in[j], hy_b_in[j], hy_conv_w[j], hy_conv_b[j],
                             hy_f_w1[j], hy_f_b1[j], hy_f_w2[j], hy_f_b2[j], hy_f_w3[j], hy_f_b3[j],
                             hy_f_w4[j], hy_f_freq[j], hy_skip[j], hy_w_out[j])
        elif m == 1:
            h = _attention_layer(h, gp, go, da_w_in[j], da_lambda[j], da_subln[j], da_w_out[j], i)
        elif m == 2:
            h = _gdn_layer(h, gp, go, gdn_w_in[j], gdn_conv_w[j], gdn_a_log[j], gdn_dt_bias[j],
                           gdn_o_norm[j], gdn_w_out[j])
        else:
            h = _conformer_layer(h, gp, go, cf_w_in[j], cf_b_in[j], cf_dw_w[j], cf_dw_b[j],
                                 cf_ln_g[j], cf_ln_b[j], cf_w_out[j], cf_b_out[j])
    return h[:, FRONT + N_META:].astype(x.dtype)
```

```python
import functools
import math

import numpy as np
import jax
import jax.numpy as jnp
from jax import lax
from jax.experimental import pallas as pl
from jax.experimental.pallas import tpu as pltpu

F32 = jnp.float32
BF16 = jnp.bfloat16

N_META = 16
FRONT = 112
HALO = 16
LANE = 128
RMS_EPS = 1e-6
LN_EPS = 1e-5
ROPE_THETA = 10000.0
VMEM_LIMIT = 56 * 1024 * 1024

HY_SHORT = 3
HY_BANDS = 16
HY_HIDDEN = 64
HY_SHORT_DECAY_PCT = 0.3
HY_LONG_DECAY_PCT = 1.5
HY_DECAY_TARGET = 1e-2
HY_FILTER_EPS = 1e-6
HY_ROWS = 64
DA_HEAD_DIM = 64
DA_V_DIM = 128
DA_NORM_EPS = 1e-5
ATTN_ANCHOR_ROWS = 16
GDN_HEAD_DIM = 128
GDN_CHUNK = 128
CF_WIDTH = 31

_NT = (((1,), (1,)), ((), ()))


def _tile(n, target, mult=8):
    best = None
    for t in range(mult, min(n, target) + 1, mult):
        if n % t == 0:
            best = t
    assert best is not None, (n, target, mult)
    return best


def _params(*sem):
    return pltpu.CompilerParams(dimension_semantics=sem, vmem_limit_bytes=VMEM_LIMIT)


def _sigmoid(x):
    return 0.5 * jnp.tanh(0.5 * x) + 0.5


def _silu(x):
    h = 0.5 * x
    return h * jnp.tanh(h) + h


def _row_valid(base, n, Tp):
    rows = base + lax.broadcasted_iota(jnp.int32, (n, 1), 0)
    return (rows >= FRONT) & (rows < Tp)


def _in_proj_kernel(h_ref, g_ref, w_ref, b_ref, o_ref):
    x = h_ref[0]
    ms = jnp.mean(x * x, axis=-1, keepdims=True)
    y = (x * lax.rsqrt(ms + RMS_EPS) * g_ref[...]).astype(BF16)
    acc = jnp.dot(y, w_ref[...], preferred_element_type=F32)
    o_ref[0] = (acc + b_ref[...]).astype(o_ref.dtype)


def _in_proj(h, gain, w, bias):
    B, Tp, D = h.shape
    N = w.shape[1]
    tm = _tile(Tp, 1056)
    tn = _tile(N, 2048, LANE)
    return pl.pallas_call(
        _in_proj_kernel,
        grid=(N // tn, B, Tp // tm),
        in_specs=[
            pl.BlockSpec((1, tm, D), lambda n, b, i: (b, i, 0)),
            pl.BlockSpec((1, D), lambda n, b, i: (0, 0)),
            pl.BlockSpec((D, tn), lambda n, b, i: (0, n)),
            pl.BlockSpec((1, tn), lambda n, b, i: (0, n)),
        ],
        out_specs=pl.BlockSpec((1, tm, tn), lambda n, b, i: (b, i, n)),
        out_shape=jax.ShapeDtypeStruct((B, Tp, N), BF16),
        compiler_params=_params("parallel", "parallel", "parallel"),
        name="in_proj",
    )(h, gain.reshape(1, D).astype(F32), w, bias.reshape(1, N).astype(F32))


def _post_norm_residual(y, g_ref, h_ref, o_ref):
    ms = jnp.mean(y * y, axis=-1, keepdims=True)
    o_ref[0] = h_ref[0] + y * lax.rsqrt(ms + RMS_EPS) * g_ref[...]


def _out_proj_kernel(a_ref, w_ref, b_ref, g_ref, h_ref, o_ref):
    y = jnp.dot(a_ref[0], w_ref[...], preferred_element_type=F32) + b_ref[...]
    _post_norm_residual(y, g_ref, h_ref, o_ref)


def _out_proj(a, w, bias, gain, h):
    B, Tp, D = h.shape
    E = a.shape[2]
    tm = _tile(Tp, 528)
    return pl.pallas_call(
        _out_proj_kernel,
        grid=(B, Tp // tm),
        in_specs=[
            pl.BlockSpec((1, tm, E), lambda b, i: (b, i, 0)),
            pl.BlockSpec((E, D), lambda b, i: (0, 0)),
            pl.BlockSpec((1, D), lambda b, i: (0, 0)),
            pl.BlockSpec((1, D), lambda b, i: (0, 0)),
            pl.BlockSpec((1, tm, D), lambda b, i: (b, i, 0)),
        ],
        out_specs=pl.BlockSpec((1, tm, D), lambda b, i: (b, i, 0)),
        out_shape=jax.ShapeDtypeStruct((B, Tp, D), F32),
        compiler_params=_params("parallel", "parallel"),
        name="out_proj",
    )(a, w, bias.reshape(1, D).astype(F32), gain.reshape(1, D).astype(F32), h)


def _halo_specs(width, tt, Tp, col):
    nt, nh, r = Tp // tt, Tp // HALO, tt // HALO
    main = pl.BlockSpec((1, tt, width), lambda b, i, *_: (b, jnp.minimum(i, nt - 1), col(*_)))
    left = pl.BlockSpec((1, HALO, width),
                        lambda b, i, *_: (b, jnp.clip(i * r - 1, 0, nh - 1), col(*_)))
    right = pl.BlockSpec((1, HALO, width),
                         lambda b, i, *_: (b, jnp.clip((i + 1) * r, 0, nh - 1), col(*_)))
    return main, left, right


CF_ROWS = 64


def _cf_kernel(am, al, ar, gm, gl, gr, z_ref, w_ref, b_ref, lg_ref, lb_ref, o_ref, G_ref, Y_ref,
               *, tt, Tp, E):
    row0 = pl.program_id(1) * tt

    def glu(a_ref, g_ref, base, n):
        a = a_ref[0].astype(F32)
        g = g_ref[0].astype(F32)
        return jnp.where(_row_valid(base, n, Tp), a * _sigmoid(g), 0.0)

    G_ref[0:HALO] = glu(al, gl, row0 - HALO, HALO)
    G_ref[HALO:HALO + tt] = glu(am, gm, row0, tt)
    G_ref[HALO + tt:HALO + tt + HALO] = glu(ar, gr, row0 + tt, HALO)

    shift = HALO - CF_WIDTH // 2
    SUB = 8

    def lane_body(c, carry):
        off = pl.multiple_of(c * LANE, LANE)
        lanes = pl.ds(off, LANE)
        bias = b_ref[:, lanes]
        taps = [w_ref[k:k + 1, lanes] for k in range(CF_WIDTH)]
        for rb in range(tt // CF_ROWS):
            acc = None
            for res in range(SUB):
                part = None
                for k in range(CF_WIDTH):
                    if (k + shift) % SUB != res:
                        continue
                    base = rb * CF_ROWS + (k + shift) - res
                    term = taps[k] * G_ref[pl.ds(base, CF_ROWS + SUB), lanes]
                    part = term if part is None else part + term
                if part is None:
                    continue
                part = part[res:res + CF_ROWS]
                acc = part if acc is None else acc + part
            Y_ref[pl.ds(rb * CF_ROWS, CF_ROWS), lanes] = acc + bias
        return carry

    lax.fori_loop(0, E // LANE, lane_body, 0)

    y = Y_ref[...]
    mu = jnp.mean(y, axis=-1, keepdims=True)
    yc = y - mu
    var = jnp.mean(yc * yc, axis=-1, keepdims=True)
    yn = yc * lax.rsqrt(var + LN_EPS) * lg_ref[...] + lb_ref[...]
    o_ref[0] = (_silu(yn) * _silu(z_ref[0].astype(F32))).astype(o_ref.dtype)


def _conformer_core(u, dw_w, dw_b, ln_g, ln_b):
    B, Tp, E3 = u.shape
    E = E3 // 3
    tt = _tile(Tp, 384, CF_ROWS)
    a_specs = _halo_specs(E, tt, Tp, lambda: 0)
    g_specs = _halo_specs(E, tt, Tp, lambda: 1)
    vec = pl.BlockSpec((1, E), lambda b, i: (0, 0))
    return pl.pallas_call(
        functools.partial(_cf_kernel, tt=tt, Tp=Tp, E=E),
        grid=(B, Tp // tt),
        in_specs=[*a_specs, *g_specs,
                  pl.BlockSpec((1, tt, E), lambda b, i: (b, i, 2)),
                  pl.BlockSpec((CF_WIDTH, E), lambda b, i: (0, 0)),
                  vec, vec, vec],
        out_specs=pl.BlockSpec((1, tt, E), lambda b, i: (b, i, 0)),
        out_shape=jax.ShapeDtypeStruct((B, Tp, E), BF16),
        scratch_shapes=[pltpu.VMEM((tt + 2 * HALO, E), F32), pltpu.VMEM((tt, E), F32)],
        compiler_params=_params("parallel", "parallel"),
        name="conformer_conv",
    )(u, u, u, u, u, u, u, dw_w.astype(F32), dw_b.reshape(1, E).astype(F32),
      ln_g.reshape(1, E).astype(F32), ln_b.reshape(1, E).astype(F32))


def _rope_kernel(u_ref, c_ref, s_ref, o_ref, *, n_q, scale):
    cos = c_ref[...]
    sin = s_ref[...]
    lane = lax.broadcasted_iota(jnp.int32, (1, LANE), 1)
    first = (lane % DA_HEAD_DIM) < DA_HEAD_DIM // 2
    for g in range(u_ref.shape[2] // LANE):
        x = u_ref[0, :, g * LANE:(g + 1) * LANE].astype(F32)
        partner = jnp.where(first, pltpu.roll(x, LANE - DA_HEAD_DIM // 2, axis=1),
                            pltpu.roll(x, DA_HEAD_DIM // 2, axis=1))
        r = x * cos + partner * sin
        if g < n_q:
            r = r * scale
        o_ref[0, :, g * LANE:(g + 1) * LANE] = r.astype(o_ref.dtype)


def _rope(u, n_qk, Tp):
    B = u.shape[0]
    half = DA_HEAD_DIM // 2
    inv_freq = ROPE_THETA ** (-jnp.arange(0, DA_HEAD_DIM, 2, dtype=F32) / DA_HEAD_DIM)
    pos = (jnp.arange(Tp, dtype=jnp.int32) - FRONT).astype(F32)
    ang = pos[:, None] * inv_freq[None, :]
    cos = jnp.tile(jnp.cos(ang), (1, LANE // half))
    sin = jnp.tile(jnp.concatenate([-jnp.sin(ang), jnp.sin(ang)], axis=1), (1, LANE // DA_HEAD_DIM))
    tm = _tile(Tp, 384)
    return pl.pallas_call(
        functools.partial(_rope_kernel, n_q=n_qk // LANE, scale=DA_HEAD_DIM ** -0.5 * math.log2(math.e)),
        grid=(B, Tp // tm),
        in_specs=[pl.BlockSpec((1, tm, 2 * n_qk), lambda b, i: (b, i, 0)),
                  pl.BlockSpec((tm, LANE), lambda b, i: (i, 0)),
                  pl.BlockSpec((tm, LANE), lambda b, i: (i, 0))],
        out_specs=pl.BlockSpec((1, tm, 2 * n_qk), lambda b, i: (b, i, 0)),
        out_shape=jax.ShapeDtypeStruct((B, Tp, 2 * n_qk), BF16),
        compiler_params=_params("parallel", "parallel"),
        name="rope",
    )(u, cos, sin)


def _attn_kernel(lv_ref, sub_ref, az_ref, q_ref, k_ref, v_ref, z_ref, o_ref, vx_ref, s_ref,
                 *, lam_init, Tp, kc, n_sub):
    first = FRONT + N_META

    @pl.when(pl.program_id(2) == 0)
    def _():
        vx_ref[:, :LANE] = v_ref[0]
        vx_ref[:, LANE:] = jnp.ones((Tp, LANE), BF16)

    lane = lax.broadcasted_iota(jnp.int32, (1, LANE), 1)
    chunks = [(0, first)] + [(r0, kc) for r0 in range(first, Tp, kc)]
    kvalid = lax.broadcasted_iota(jnp.int32, (1, first), 1) >= FRONT
    nc = len(chunks)
    n_acc = 2
    lv = lv_ref[...]
    lam = (jnp.exp(jnp.sum(lv[0:1] * lv[1:2], axis=-1, keepdims=True))
           - jnp.exp(jnp.sum(lv[2:3] * lv[3:4], axis=-1, keepdims=True)) + lam_init)

    def row_max(first_chunk, part):
        return jnp.maximum(jnp.max(part, axis=-1, keepdims=True),
                           jnp.max(first_chunk, axis=-1, keepdims=True))

    def accumulate(accs, t, c):
        accs[c % n_acc] = t if accs[c % n_acc] is None else accs[c % n_acc] + t

    tq = q_ref.shape[1]
    ts = tq // n_sub
    anchor = {}
    qmask = {}
    stats = {}

    def masked_q(t):
        if t not in qmask:
            q = q_ref[0, t * ts:(t + 1) * ts, :]
            if t - 1 in anchor:
                q = q + jnp.tile(anchor[t - 1], (ts // ATTN_ANCHOR_ROWS, 1))
            zero = jnp.zeros_like(q)
            qmask[t] = [jnp.where(lane < DA_HEAD_DIM, q, zero), jnp.where(lane >= DA_HEAD_DIM, q, zero)]
        return qmask[t]

    def scores(t, m, c):
        r0, n = chunks[c]
        s = lax.dot_general(masked_q(t)[m], k_ref[0, r0:r0 + n, :], _NT, preferred_element_type=F32)
        st = stats.setdefault((t, m), [None, None])
        if c == 0:
            s = jnp.where(kvalid, s, -1e30)
            st[0] = s
        else:
            st[1] = s if st[1] is None else jnp.maximum(st[1], s)
        s_ref[2 * (t % 2) + m, :, r0:r0 + n] = s

    def weighted(t, m, mx, c):
        r0, n = chunks[c]
        p = jnp.exp2((s_ref[2 * (t % 2) + m, :, r0:r0 + n] - mx).astype(BF16))
        if m == 0 and c == 0:
            anchor[t] = p[0:ATTN_ANCHOR_ROWS, 0:LANE] * az_ref[...]
        return jnp.dot(p, vx_ref[r0:r0 + n, :], preferred_element_type=F32)

    for m in range(2):
        for c in range(nc):
            scores(0, m, c)
    for t in range(n_sub):
        rows = slice(t * ts, (t + 1) * ts)
        outs = []
        for m in range(2):
            mx = row_max(*stats[(t, m)])
            accs = [None] * n_acc
            for c in range(nc):
                accumulate(accs, weighted(t, m, mx, c), c)
                if t + 1 < n_sub:
                    scores(t + 1, m, c)
            acc = functools.reduce(lambda x, y: x + y, accs)
            outs.append(acc[:, :LANE] / acc[:, LANE:LANE + 1])
        o = outs[0] - lam * outs[1]
        ms = jnp.mean(o * o, axis=-1, keepdims=True)
        o = o * lax.rsqrt(ms + DA_NORM_EPS) * sub_ref[...] * (1.0 - lam_init)
        o_ref[0, rows, :] = (o * _silu(z_ref[0, rows, :].astype(F32))).astype(o_ref.dtype)


def _diff_attention_core(u, qk, lam_vecs, subln, layer_idx, E):
    B, Tp, _ = u.shape
    H = E // DA_V_DIM
    tq = _tile(Tp, 1056)
    kc = 2 * LANE
    assert (Tp - FRONT - N_META) % kc == 0
    lam_init = 0.8 - 0.6 * math.exp(-0.3 * layer_idx)
    n_sub = max(n for n in range(1, 9) if tq % (ATTN_ANCHOR_ROWS * n) == 0)
    anchor_zeros = jnp.zeros((ATTN_ANCHOR_ROWS, LANE), BF16)
    return pl.pallas_call(
        functools.partial(_attn_kernel, lam_init=lam_init, Tp=Tp, kc=kc, n_sub=n_sub),
        grid=(B, H, Tp // tq),
        in_specs=[
            pl.BlockSpec((4, DA_HEAD_DIM), lambda b, h, i: (0, 0)),
            pl.BlockSpec((1, DA_V_DIM), lambda b, h, i: (0, 0)),
            pl.BlockSpec((ATTN_ANCHOR_ROWS, LANE), lambda b, h, i: (0, 0)),
            pl.BlockSpec((1, tq, LANE), lambda b, h, i: (b, i, h)),
            pl.BlockSpec((1, Tp, LANE), lambda b, h, i: (b, 0, H + h)),
            pl.BlockSpec((1, Tp, LANE), lambda b, h, i: (b, 0, 2 * H + h)),
            pl.BlockSpec((1, tq, LANE), lambda b, h, i: (b, i, 3 * H + h)),
        ],
        out_specs=pl.BlockSpec((1, tq, LANE), lambda b, h, i: (b, i, h)),
        out_shape=jax.ShapeDtypeStruct((B, Tp, E), BF16),
        scratch_shapes=[pltpu.VMEM((Tp, 2 * LANE), BF16), pltpu.VMEM((4, tq // n_sub, Tp), F32)],
        compiler_params=_params("parallel", "parallel", "arbitrary"),
        name="diff_attention",
    )(lam_vecs.astype(F32), subln.reshape(1, DA_V_DIM).astype(F32), anchor_zeros, qk, qk, u, u)


def _gdn_prep_kernel(um, ul, ur, ab_ref, cw_ref, al_ref, dt_ref,
                     q_ref, k_ref, kt_ref, v_ref, gb_ref, gbt_ref, U_ref, *, tt, Tp, n_qk, n_g):
    row0 = pl.program_id(1) * tt

    def masked(ref, base, n):
        return jnp.where(_row_valid(base, n, Tp), ref[0].astype(F32), 0.0)

    U_ref[0:HALO] = masked(ul, row0 - HALO, HALO)
    U_ref[HALO:HALO + tt] = masked(um, row0, tt)
    U_ref[HALO + tt:HALO + tt + HALO] = masked(ur, row0 + tt, HALO)
    valid = _row_valid(row0, tt, Tp)

    def conv_silu(c0, width):
        cols = slice(c0, c0 + width)
        y = (cw_ref[0:1, cols] * U_ref[HALO - 1:HALO - 1 + tt, cols]
             + cw_ref[1:2, cols] * U_ref[HALO:HALO + tt, cols]
             + cw_ref[2:3, cols] * U_ref[HALO + 1:HALO + 1 + tt, cols])
        return _silu(y)

    def l2n(x):
        return x * lax.rsqrt(jnp.sum(x * x, axis=-1, keepdims=True) + 1e-6)

    for hh in range(n_qk // LANE):
        c0 = hh * LANE
        qh = l2n(conv_silu(c0, LANE)) * GDN_HEAD_DIM ** -0.5
        q_ref[0, :, c0:c0 + LANE] = jnp.where(valid, qh, 0.0).astype(q_ref.dtype)
        kh = jnp.where(valid, l2n(conv_silu(n_qk + c0, LANE)), 0.0)
        k_ref[0, :, c0:c0 + LANE] = kh.astype(k_ref.dtype)
        kt_ref[0, c0:c0 + LANE, :] = kh.T.astype(kt_ref.dtype)
    n_v = v_ref.shape[2]
    for c0 in range(0, n_v, LANE):
        vh = conv_silu(2 * n_qk + c0, LANE)
        v_ref[0, :, c0:c0 + LANE] = jnp.where(valid, vh, 0.0).astype(v_ref.dtype)

    x = ab_ref[0].astype(F32)
    lane = lax.broadcasted_iota(jnp.int32, (1, LANE), 1)
    xa = x + dt_ref[...]
    softplus = jnp.maximum(xa, 0.0) + jnp.log(1.0 + jnp.exp(-jnp.abs(xa)))
    g = -jnp.exp(al_ref[...]) * softplus
    beta = _sigmoid(x)
    gb = jnp.where(lane < n_g, g, jnp.where(lane < 2 * n_g, beta, 0.0))
    gb = jnp.where(valid, gb, 0.0)

    H = n_g // 2
    C = GDN_CHUNK
    r = lax.broadcasted_iota(jnp.int32, (C, C), 0)
    c = lax.broadcasted_iota(jnp.int32, (C, C), 1)
    lower = (c <= r).astype(F32)
    upper = (c >= r).astype(F32)
    hi = lax.Precision.HIGHEST

    def pack(gc, bt, tot):
        return jnp.where(lane < H, gc, jnp.where(lane < 2 * H, bt, jnp.where(lane < 3 * H, tot, 0.0)))

    for c3 in range(tt // C):
        rows = slice(c3 * C, (c3 + 1) * C)
        gch = gb[rows]
        cf = jnp.dot(lower, gch, precision=hi, preferred_element_type=F32)
        cb = jnp.dot(upper, gch, precision=hi, preferred_element_type=F32)
        tot = jnp.broadcast_to(jnp.sum(gch, axis=0, keepdims=True), (C, LANE))
        g0 = pack(cf, pltpu.roll(gch, LANE - H, axis=1), pltpu.roll(tot, 2 * H, axis=1))
        g1 = pack(pltpu.roll(cb, LANE - H, axis=1), pltpu.roll(gch, LANE - 2 * H, axis=1),
                  pltpu.roll(tot, H, axis=1))
        gb_ref[0, 0, rows, :] = g0
        gb_ref[1, 0, rows, :] = g1
        gbt_ref[0, 0, :, rows] = g0.T
        gbt_ref[1, 0, :, rows] = g1.T


def _gdn_prep(u, conv_w, a_log, dt_bias, n_qk, E):
    B, Tp, _ = u.shape
    n_conv = 2 * n_qk + E
    n_g = a_log.size
    assert 3 * (n_g // 2) <= LANE
    tt = _tile(Tp, 384, GDN_CHUNK)
    specs = _halo_specs(n_conv, tt, Tp, lambda: 0)
    al = jnp.zeros((1, LANE), F32).at[0, :n_g].set(a_log.reshape(-1).astype(F32))
    dt = jnp.zeros((1, LANE), F32).at[0, :n_g].set(dt_bias.reshape(-1).astype(F32))
    vec = pl.BlockSpec((1, LANE), lambda b, i: (0, 0))
    return pl.pallas_call(
        functools.partial(_gdn_prep_kernel, tt=tt, Tp=Tp, n_qk=n_qk, n_g=n_g),
        grid=(B, Tp // tt),
        in_specs=[*specs,
                  pl.BlockSpec((1, tt, LANE), lambda b, i: (b, i, (n_conv + E) // LANE)),
                  pl.BlockSpec((HY_SHORT, n_conv), lambda b, i: (0, 0)),
                  vec, vec],
        out_specs=[
            pl.BlockSpec((1, tt, n_qk), lambda b, i: (b, i, 0)),
            pl.BlockSpec((1, tt, n_qk), lambda b, i: (b, i, 0)),
            pl.BlockSpec((1, n_qk, tt), lambda b, i: (b, 0, i)),
            pl.BlockSpec((1, tt, E), lambda b, i: (b, i, 0)),
            pl.BlockSpec((2, 1, tt, LANE), lambda b, i: (0, b, i, 0)),
            pl.BlockSpec((2, 1, LANE, tt), lambda b, i: (0, b, 0, i)),
        ],
        out_shape=[
            jax.ShapeDtypeStruct((B, Tp, n_qk), BF16),
            jax.ShapeDtypeStruct((B, Tp, n_qk), BF16),
            jax.ShapeDtypeStruct((B, n_qk, Tp), BF16),
            jax.ShapeDtypeStruct((B, Tp, E), BF16),
            jax.ShapeDtypeStruct((2, B, Tp, LANE), F32),
            jax.ShapeDtypeStruct((2, B, LANE, Tp), F32),
        ],
        scratch_shapes=[pltpu.VMEM((tt + 2 * HALO, n_conv), F32)],
        compiler_params=_params("parallel", "parallel"),
        name="gdn_prep",
    )(u, u, u, u, conv_w.astype(F32), al, dt)


def _mm(a, b):
    return jnp.dot(a.astype(BF16), b.astype(BF16), preferred_element_type=F32)


def _unit_lower_inverse(mats):
    C = mats[0].shape[0]
    row = lax.broadcasted_iota(jnp.int32, (C, C), 0)
    col = lax.broadcasted_iota(jnp.int32, (C, C), 1)
    rc = row ^ col
    ps = [jnp.where(rc < 8, a, 0.0) for a in mats]
    xs = [jnp.where(rc == 0, 1.0, 0.0) - p for p in ps]
    n = 2
    while n < 8:
        ps = [_mm(p, p) for p in ps]
        xs = [x + _mm(x, p) for x, p in zip(xs, ps)]
        n *= 2
    b = 8
    while b < C:
        sel = (rc >= b) & (rc < 2 * b)
        ys = [_mm(x, jnp.where(sel, a, 0.0)) for x, a in zip(xs, mats)]
        xs = [x - _mm(y, x) for x, y in zip(xs, ys)]
        b *= 2
    return xs


def _gdn_chunk_kernel(q_ref, k_ref, kt_ref, v_ref, gb_ref, gbt_ref, o_ref, S_ref, *, n_heads):
    d = pl.program_id(0)

    @pl.when(pl.program_id(2) == 0)
    def _():
        S_ref[...] = jnp.zeros_like(S_ref)

    C = q_ref.shape[1]
    H = n_heads
    rep = n_heads // (q_ref.shape[2] // GDN_HEAD_DIM)
    row = lax.broadcasted_iota(jnp.int32, (C, C), 0)
    col = lax.broadcasted_iota(jnp.int32, (C, C), 1)
    ahead = (row - col) * (1 - 2 * d)
    incl = ahead >= 0
    strict = ahead > 0
    gb = gb_ref[0, 0]
    gbt = gbt_ref[0, 0]

    heads = range(n_heads)
    kcols = [slice((hv // rep) * LANE, (hv // rep + 1) * LANE) for hv in heads]
    kk, qk = [], []
    for kh in range(n_heads // rep):
        cols = slice(kh * LANE, (kh + 1) * LANE)
        k = k_ref[0, :, cols]
        kk.append(lax.dot_general(k, k, _NT, preferred_element_type=F32))
        qk.append(lax.dot_general(q_ref[0, :, cols], k, _NT, preferred_element_type=F32))

    gc = [gb[:, hv:hv + 1] for hv in heads]
    beta = [gb[:, H + hv:H + hv + 1] for hv in heads]
    gc_row = [gbt[hv:hv + 1, :] for hv in heads]
    gtot_row = [gbt[2 * H + hv:2 * H + hv + 1, :] for hv in heads]
    decay = [jnp.exp(jnp.where(incl, gc[hv] - gc_row[hv], -1e30)) for hv in heads]
    a = [jnp.where(strict, beta[hv] * kk[hv // rep] * decay[hv], 0.0) for hv in heads]
    t_inv = _unit_lower_inverse(a)
    e_gc = [jnp.exp(g) for g in gc]
    rhs = [jnp.concatenate([v_ref[0, :, hv * LANE:(hv + 1) * LANE].astype(F32) * beta[hv],
                            k_ref[0, :, kcols[hv]].astype(F32) * (beta[hv] * e_gc[hv])], axis=1)
           for hv in heads]
    sol = [_mm(t_inv[hv], rhs[hv]) for hv in heads]
    s_old = [S_ref[hv] for hv in heads]
    r = [_mm(jnp.concatenate([sol[hv][:, LANE:], q_ref[0, :, kcols[hv]].astype(F32) * e_gc[hv]], axis=0),
             s_old[hv]) for hv in heads]
    v_new = [(sol[hv][:, :LANE] - r[hv][:C]).astype(BF16) for hv in heads]
    for hv in heads:
        o = r[hv][C:] + _mm(qk[hv // rep] * decay[hv], v_new[hv])
        o_ref[0, 0, :, hv * LANE:(hv + 1) * LANE] = o.astype(o_ref.dtype)
    for hv in heads:
        k_end_t = kt_ref[0, kcols[hv], :].astype(F32) * jnp.exp(gtot_row[hv] - gc_row[hv])
        S_ref[hv] = s_old[hv] * jnp.exp(gtot_row[hv][:, 0:1]) + _mm(k_end_t, v_new[hv])


def _gdn_scan(q, k, kt, v, gb, gbt):
    B, Tp, n_qk = q.shape
    E = v.shape[2]
    n_heads = E // GDN_HEAD_DIM
    C = GDN_CHUNK
    nC = Tp // C

    def cc(d, c):
        return c + d * (nC - 1 - 2 * c)

    return pl.pallas_call(
        functools.partial(_gdn_chunk_kernel, n_heads=n_heads),
        grid=(2, B, nC),
        in_specs=[
            pl.BlockSpec((1, C, n_qk), lambda d, b, c: (b, cc(d, c), 0)),
            pl.BlockSpec((1, C, n_qk), lambda d, b, c: (b, cc(d, c), 0)),
            pl.BlockSpec((1, n_qk, C), lambda d, b, c: (b, 0, cc(d, c))),
            pl.BlockSpec((1, C, E), lambda d, b, c: (b, cc(d, c), 0)),
            pl.BlockSpec((1, 1, C, LANE), lambda d, b, c: (d, b, cc(d, c), 0)),
            pl.BlockSpec((1, 1, LANE, C), lambda d, b, c: (d, b, 0, cc(d, c))),
        ],
        out_specs=pl.BlockSpec((1, 1, C, E), lambda d, b, c: (d, b, cc(d, c), 0)),
        out_shape=jax.ShapeDtypeStruct((2, B, Tp, E), BF16),
        scratch_shapes=[pltpu.VMEM((n_heads, GDN_HEAD_DIM, GDN_HEAD_DIM), F32)],
        compiler_params=_params("parallel", "parallel", "arbitrary"),
        name="gdn_scan",
    )(q, k, kt, v, gb, gbt)


def _gdn_out_kernel(of_ref, ob_ref, z_ref, on_ref, w_ref, g_ref, h_ref, o_ref):
    E = of_ref.shape[3]
    parts = []
    for c0 in range(0, E, LANE):
        o = of_ref[0, 0, :, c0:c0 + LANE].astype(F32) + ob_ref[0, 0, :, c0:c0 + LANE].astype(F32)
        ms = jnp.mean(o * o, axis=-1, keepdims=True)
        o = o * lax.rsqrt(ms + RMS_EPS) * on_ref[...]
        parts.append((o * _silu(z_ref[0, :, c0:c0 + LANE].astype(F32))).astype(BF16))
    a = jnp.concatenate(parts, axis=1)
    y = jnp.dot(a, w_ref[...], preferred_element_type=F32)
    _post_norm_residual(y, g_ref, h_ref, o_ref)


def _gdn_out(o2, u, z_col, o_norm, w, gain, h):
    B, Tp, D = h.shape
    E = o2.shape[3]
    tm = _tile(Tp, 528)
    return pl.pallas_call(
        _gdn_out_kernel,
        grid=(B, Tp // tm),
        in_specs=[
            pl.BlockSpec((1, 1, tm, E), lambda b, i: (0, b, i, 0)),
            pl.BlockSpec((1, 1, tm, E), lambda b, i: (1, b, i, 0)),
            pl.BlockSpec((1, tm, E), lambda b, i: (b, i, z_col)),
            pl.BlockSpec((1, GDN_HEAD_DIM), lambda b, i: (0, 0)),
            pl.BlockSpec((E, D), lambda b, i: (0, 0)),
            pl.BlockSpec((1, D), lambda b, i: (0, 0)),
            pl.BlockSpec((1, tm, D), lambda b, i: (b, i, 0)),
        ],
        out_specs=pl.BlockSpec((1, tm, D), lambda b, i: (b, i, 0)),
        out_shape=jax.ShapeDtypeStruct((B, Tp, D), F32),
        compiler_params=_params("parallel", "parallel"),
        name="gdn_out",
    )(o2, o2, u, o_norm.reshape(1, GDN_HEAD_DIM).astype(F32), w, gain.reshape(1, D).astype(F32), h)


def _hy_block(Tp):
    P = 768 if Tp >= 1536 else 128
    return P, -(-Tp // P)


def _dft_mats(P):
    k = np.arange(P, dtype=np.int64)[:, None]
    b = np.arange(P, dtype=np.int64)[None, :]

    def mat(m):
        ang = np.pi * (((2 * k + 1) * m) % (4 * P)).astype(np.float64) / (2 * P)
        return np.concatenate([np.cos(ang), -np.sin(ang)], axis=0).astype(np.float32)

    fwd = mat(b)
    lo = mat(b - P)
    as_bf16 = lambda a: jnp.asarray(a).astype(BF16)
    return as_bf16(fwd), as_bf16(lo), as_bf16(np.ascontiguousarray(fwd.T))


def _hy_filter_kernel(band_ref, w1_ref, b1_ref, w2_ref, b2_ref, w3_ref, b3_ref, fr_ref,
                      w4_ref, w4b_ref, dl_ref, g_ref, n_ref, *, P, J, T):
    i = pl.program_id(0)
    hi = lax.Precision.HIGHEST
    d = i * P + lax.broadcasted_iota(jnp.int32, (P, 1), 0) - J * P
    s = jnp.abs(d).astype(F32)
    t = s / (T - 1)
    w = (2.0 * math.pi) * s / T
    lane = lax.broadcasted_iota(jnp.int32, (1, LANE), 1)
    arg = band_ref[...] * w
    z = jnp.where(lane == 0, t,
                  jnp.where(lane <= HY_BANDS, jnp.cos(arg),
                            jnp.where(lane <= 2 * HY_BANDS, -jnp.sin(arg), 0.0)))
    fr = fr_ref[...]
    hdn = jnp.sin(fr * (jnp.dot(z, w1_ref[...], precision=hi, preferred_element_type=F32) + b1_ref[...]))
    hdn = jnp.sin(fr * (jnp.dot(hdn, w2_ref[...], precision=hi, preferred_element_type=F32) + b2_ref[...]))
    hdn = jnp.sin(fr * (jnp.dot(hdn, w3_ref[...], precision=hi, preferred_element_type=F32) + b3_ref[...]))
    filt = jnp.dot(hdn, w4_ref[...], precision=hi, preferred_element_type=F32) * jnp.exp(-t * dl_ref[...])
    g = jnp.where(s < T, filt, 0.0)
    g_ref[...] = g
    part = jnp.sum(jnp.abs(g), axis=0, keepdims=True)

    @pl.when(i == 0)
    def _():
        n_ref[...] = jnp.zeros_like(n_ref)

    n_ref[...] += part

    @pl.when(i == J)
    def _():
        extra = jnp.dot(hdn[0:8], w4b_ref[...], precision=hi, preferred_element_type=F32)
        n_ref[...] += jnp.abs(extra[0:1])


def _hy_filter(f_w1, f_b1, f_w2, f_b2, f_w3, f_b3, f_w4, f_freq, E, T, P, J):
    pad2 = lambda a: jnp.zeros((LANE, LANE), F32).at[:a.shape[0], :a.shape[1]].set(a.astype(F32))
    padv = lambda a: jnp.zeros((1, LANE), F32).at[0, :a.shape[0]].set(a.astype(F32))
    bands = jnp.linspace(1e-4, HY_BANDS - 1, HY_BANDS, dtype=F32)
    band = jnp.zeros((1, LANE), F32).at[0, 1:1 + HY_BANDS].set(bands).at[0, 1 + HY_BANDS:1 + 2 * HY_BANDS].set(bands)
    w4 = jnp.zeros((LANE, 2 * E), F32).at[:HY_HIDDEN].set(f_w4.astype(F32))
    max_decay = math.log(HY_DECAY_TARGET) / HY_SHORT_DECAY_PCT
    min_decay = math.log(HY_DECAY_TARGET) / HY_LONG_DECAY_PCT
    deltas = jnp.abs(jnp.linspace(min_decay, max_decay, E, dtype=F32)).reshape(1, E)
    sq = pl.BlockSpec((LANE, LANE), lambda i: (0, 0))
    vec = pl.BlockSpec((1, LANE), lambda i: (0, 0))
    return pl.pallas_call(
        functools.partial(_hy_filter_kernel, P=P, J=J, T=T),
        grid=(2 * J,),
        in_specs=[vec, sq, vec, sq, vec, sq, vec, vec,
                  pl.BlockSpec((LANE, E), lambda i: (0, jnp.where(i >= J, 0, 1))),
                  pl.BlockSpec((LANE, E), lambda i: (0, 1)),
                  pl.BlockSpec((1, E), lambda i: (0, 0))],
        out_specs=[pl.BlockSpec((P, E), lambda i: (i, 0)),
                   pl.BlockSpec((1, E), lambda i: (0, 0))],
        out_shape=[jax.ShapeDtypeStruct((2 * J * P, E), F32),
                   jax.ShapeDtypeStruct((1, E), F32)],
        compiler_params=_params("arbitrary"),
        name="hyena_filter",
    )(band, pad2(f_w1), padv(f_b1), pad2(f_w2), padv(f_b2), pad2(f_w3), padv(f_b3), padv(f_freq),
      w4, w4, deltas)


def _hy_spec_kernel(lo_ref, hi_ref, flo_ref, fhi_ref, n_ref, gr_ref, gi_ref, *, P):
    spec = (jnp.dot(flo_ref[...], lo_ref[...].astype(BF16), preferred_element_type=F32)
            + jnp.dot(fhi_ref[...], hi_ref[...].astype(BF16), preferred_element_type=F32))
    scale = 1.0 / ((n_ref[...] + HY_FILTER_EPS) * P)
    gr_ref[0] = spec[:P] * scale
    gi_ref[0] = spec[P:] * scale


def _hy_spectrum(g, norm, fwd, lo, P, J):
    E = g.shape[1]
    cs = _tile(E, 256, LANE)
    mat = pl.BlockSpec((2 * P, P), lambda c, q: (0, 0))
    out = pl.BlockSpec((1, P, cs), lambda c, q: (q, 0, c))
    shape = jax.ShapeDtypeStruct((2 * J - 1, P, E), F32)
    return pl.pallas_call(
        functools.partial(_hy_spec_kernel, P=P),
        grid=(E // cs, 2 * J - 1),
        in_specs=[pl.BlockSpec((P, cs), lambda c, q: (q, c)),
                  pl.BlockSpec((P, cs), lambda c, q: (q + 1, c)),
                  mat, mat,
                  pl.BlockSpec((1, cs), lambda c, q: (0, c))],
        out_specs=[out, out],
        out_shape=[shape, shape],
        compiler_params=_params("parallel", "parallel"),
        name="hyena_spectrum",
    )(g, g, lo, fwd, norm)


def _hy_prep_kernel(x0m, x0l, x0r, x1m, x1l, x1r, vm, vl, vr, z_ref, cw_ref, cb_ref,
                    w_ref, gg_ref, U_ref, *, tt, Tp, ce):
    row0 = pl.program_id(1) * tt
    valid = _row_valid(row0, tt, Tp)

    def conv(k, main, left, right):
        def masked(ref, base, n):
            return jnp.where(_row_valid(base, n, Tp), ref[0].astype(F32), 0.0)

        U_ref[0:HALO] = masked(left, row0 - HALO, HALO)
        U_ref[HALO:HALO + tt] = masked(main, row0, tt)
        U_ref[HALO + tt:HALO + tt + HALO] = masked(right, row0 + tt, HALO)
        cw = cw_ref[k]
        return (cw[0:1] * U_ref[HALO - 1:HALO - 1 + tt] + cw[1:2] * U_ref[HALO:HALO + tt]
                + cw[2:3] * U_ref[HALO + 1:HALO + 1 + tt] + cb_ref[k])

    x1 = conv(1, x1m, x1l, x1r)
    v = conv(2, vm, vl, vr)
    w_ref[0] = jnp.where(valid, v * x1, 0.0)
    x0 = conv(0, x0m, x0l, x0r)
    gg_ref[0] = jnp.where(valid, x0 * _silu(z_ref[0].astype(F32)), 0.0).astype(gg_ref.dtype)


def _hy_prep(u, conv_w, conv_b, E, TH):
    B, Tp, _ = u.shape
    ce = _tile(E, 512, LANE)
    nce = E // ce
    tt = _tile(math.gcd(Tp, TH), 384, HALO)
    specs = []
    for s in range(3):
        specs += _halo_specs(ce, tt, Tp, lambda c, s=s: s * nce + c)
    nt = Tp // tt
    cw = conv_w.astype(F32).reshape(HY_SHORT, 3, E).transpose(1, 0, 2)
    cb = conv_b.astype(F32).reshape(3, 1, E)
    return pl.pallas_call(
        functools.partial(_hy_prep_kernel, tt=tt, Tp=Tp, ce=ce),
        grid=(B, TH // tt, nce),
        in_specs=[*specs,
                  pl.BlockSpec((1, tt, ce), lambda b, i, c: (b, jnp.minimum(i, nt - 1), 3 * nce + c)),
                  pl.BlockSpec((3, HY_SHORT, ce), lambda b, i, c: (0, 0, c)),
                  pl.BlockSpec((3, 1, ce), lambda b, i, c: (0, 0, c))],
        out_specs=[pl.BlockSpec((1, tt, ce), lambda b, i, c: (b, i, c)),
                   pl.BlockSpec((1, tt, ce), lambda b, i, c: (b, i, c))],
        out_shape=[jax.ShapeDtypeStruct((B, TH, E), F32),
                   jax.ShapeDtypeStruct((B, TH, E), BF16)],
        scratch_shapes=[pltpu.VMEM((tt + 2 * HALO, ce), F32)],
        compiler_params=_params("parallel", "parallel", "parallel"),
        name="hyena_prep",
    )(*([u] * 10), cw, cb)


def _hy_conv_kernel(w_ref, gg_ref, fwd_ref, inv_ref, gr_ref, gi_ref, sk_ref, o_ref, yr_ref, yi_ref,
                    *, P, J):
    s = pl.program_id(2)
    nb = w_ref.shape[0]

    def forward(first_block):
        wv = jnp.concatenate([w_ref[b] for b in range(nb)], axis=1).astype(BF16)
        spec = jnp.dot(fwd_ref[...], wv, preferred_element_type=F32)
        for r0 in range(0, P, HY_ROWS):
            rows = slice(r0, r0 + HY_ROWS)
            vr, vi = spec[rows], spec[P + r0:P + r0 + HY_ROWS]
            for i in range(J):
                q = i - s + J - 1
                gr = jnp.concatenate([gr_ref[q, rows, :]] * nb, axis=1)
                gi = jnp.concatenate([gi_ref[q, rows, :]] * nb, axis=1)
                yr, yi = vr * gr - vi * gi, vr * gi + vi * gr
                if first_block:
                    yr_ref[i, rows, :] = yr
                    yi_ref[i, rows, :] = yi
                else:
                    yr_ref[i, rows, :] += yr
                    yi_ref[i, rows, :] += yi

    @pl.when(s == 0)
    def _():
        forward(True)

    @pl.when((s > 0) & (s < J))
    def _():
        forward(False)

    @pl.when(s >= J)
    def _():
        i = s - J
        spec = jnp.concatenate([yr_ref[i], yi_ref[i]], axis=0).astype(BF16)
        y = jnp.dot(inv_ref[...], spec, preferred_element_type=F32)
        for b in range(nb):
            yb = y[:, b * LANE:(b + 1) * LANE] + sk_ref[...] * w_ref[b]
            o_ref[b] = (yb * gg_ref[b].astype(F32)).astype(o_ref.dtype)


def _hy_long_conv(w, gg, fwd, inv, gr, gi, skip, P, J):
    B, TH, E = w.shape
    nb = max(n for n in (1, 2, 4) if B % n == 0)
    blk = lambda fn: pl.BlockSpec((nb, P, LANE), fn)
    once = pl.Buffered(1)
    spectra = pl.BlockSpec((2 * J - 1, P, LANE), lambda c, b, s: (0, 0, c), pipeline_mode=once)
    return pl.pallas_call(
        functools.partial(_hy_conv_kernel, P=P, J=J),
        grid=(E // LANE, B // nb, 2 * J),
        in_specs=[blk(lambda c, b, s: (b, jnp.where(s < J, s, s - J), c)),
                  blk(lambda c, b, s: (b, jnp.maximum(s - J, 0), c)),
                  pl.BlockSpec((2 * P, P), lambda c, b, s: (0, 0), pipeline_mode=once),
                  pl.BlockSpec((P, 2 * P), lambda c, b, s: (0, 0), pipeline_mode=once),
                  spectra, spectra,
                  pl.BlockSpec((1, LANE), lambda c, b, s: (0, c))],
        out_specs=blk(lambda c, b, s: (b, jnp.maximum(s - J, 0), c)),
        out_shape=jax.ShapeDtypeStruct((B, TH, E), BF16),
        scratch_shapes=[pltpu.VMEM((J, P, nb * LANE), F32), pltpu.VMEM((J, P, nb * LANE), F32)],
        compiler_params=_params("parallel", "parallel", "arbitrary"),
        name="hyena_long_conv",
    )(w, gg, fwd, inv, gr, gi, skip.reshape(1, E).astype(F32))


def _hyena_layer(h, gp, go, w_in, b_in, conv_w, conv_b, f_w1, f_b1, f_w2, f_b2, f_w3, f_b3, f_w4,
                 f_freq, skip, w_out):
    B, Tp, D = h.shape
    E = w_out.shape[0]
    T = Tp - FRONT
    P, J = _hy_block(Tp)
    u = _in_proj(h, gp, w_in.astype(BF16), b_in)
    g, norm = _hy_filter(f_w1, f_b1, f_w2, f_b2, f_w3, f_b3, f_w4, f_freq, E, T, P, J)
    fwd, lo, inv = _dft_mats(P)
    gr, gi = _hy_spectrum(g, norm, fwd, lo, P, J)
    w, gg = _hy_prep(u, conv_w, conv_b, E, J * P)
    a = _hy_long_conv(w, gg, fwd, inv, gr, gi, skip, P, J)
    return _out_proj(a, w_out.astype(BF16), jnp.zeros((D,), F32), go, h)


def _attention_layer(h, gp, go, w_in, lam_vecs, subln, w_out, layer_idx):
    B, Tp, D = h.shape
    E = w_out.shape[0]
    n_qk = (w_in.shape[1] - 2 * E) // 2
    u = _in_proj(h, gp, w_in.astype(BF16), jnp.zeros((w_in.shape[1],), F32))
    qk = _rope(u, n_qk, Tp)
    a = _diff_attention_core(u, qk, lam_vecs, subln, layer_idx, E)
    return _out_proj(a, w_out.astype(BF16), jnp.zeros((D,), F32), go, h)


def _gdn_layer(h, gp, go, w_in, conv_w, a_log, dt_bias, o_norm, w_out):
    B, Tp, D = h.shape
    E = w_out.shape[0]
    n_in = w_in.shape[1]
    n_conv = conv_w.shape[1]
    n_qk = (n_conv - E) // 2
    n_pad = -(-n_in // LANE) * LANE
    w_p = jnp.zeros((D, n_pad), BF16).at[:, :n_in].set(w_in.astype(BF16))
    u = _in_proj(h, gp, w_p, jnp.zeros((n_pad,), F32))
    q, k, kt, v, gb, gbt = _gdn_prep(u, conv_w, a_log, dt_bias, n_qk, E)
    o2 = _gdn_scan(q, k, kt, v, gb, gbt)
    return _gdn_out(o2, u, n_conv // E, o_norm, w_out.astype(BF16), go, h)


def _conformer_layer(h, gp, go, w_in, b_in, dw_w, dw_b, ln_g, ln_b, w_out, b_out):
    u = _in_proj(h, gp, w_in.astype(BF16), b_in)
    a = _conformer_core(u, dw_w, dw_b, ln_g, ln_b)
    return _out_proj(a, w_out.astype(BF16), b_out, go, h)


def kernel(x, meta, norm_pre, norm_post, hy_w_in, hy_b_in, hy_conv_w, hy_conv_b, hy_f_w1, hy_f_b1, hy_f_w2, hy_f_b2, hy_f_w3, hy_f_b3, hy_f_w4, hy_f_freq, hy_skip, hy_w_out, da_w_in, da_lambda, da_subln, da_w_out, gdn_w_in, gdn_conv_w, gdn_a_log, gdn_dt_bias, gdn_o_norm, gdn_w_out, cf_w_in, cf_b_in, cf_dw_w, cf_dw_b, cf_ln_g, cf_ln_b, cf_w_out, cf_b_out):
    B, S, D = x.shape
    assert S % LANE == 0
    h = jnp.concatenate([jnp.zeros((B, FRONT, D), F32),
                         jnp.broadcast_to(meta[None].astype(F32), (B, N_META, D)),
                         x.astype(F32)], axis=1)
    for i in range(norm_pre.shape[0]):
        m, j = i % 4, i // 4
        gp, go = norm_pre[i], norm_post[i]
        if m == 0:
            h = _hyena_layer(h, gp, go, hy_w_in[j], hy_b_in[j], hy_conv_w[j], hy_conv_b[j],
                             hy_f_w1[j], hy_f_b1[j], hy_f_w2[j], hy_f_b2[j], hy_f_w3[j], hy_f_b3[j],
                             hy_f_w4[j], hy_f_freq[j], hy_skip[j], hy_w_out[j])
        elif m == 1:
            h = _attention_layer(h, gp, go, da_w_in[j], da_lambda[j], da_subln[j], da_w_out[j], i)
        elif m == 2:
            h = _gdn_layer(h, gp, go, gdn_w_in[j], gdn_conv_w[j], gdn_a_log[j], gdn_dt_bias[j],
                           gdn_o_norm[j], gdn_w_out[j])
        else:
            h = _conformer_layer(h, gp, go, cf_w_in[j], cf_b_in[j], cf_dw_w[j], cf_dw_b[j],
                                 cf_ln_g[j], cf_ln_b[j], cf_w_out[j], cf_b_out[j])
    return h[:, FRONT + N_META:].astype(x.dtype)
```
